```python
import math
import jax, jax.numpy as jnp
from jax import lax
import numpy as np

D_MODEL = 1024
BATCH = 8
SEQ = 2048
DEPTH = 4
DEC_BATCH = 128
DEC_SEQ = 1
PAST_LEN = 16384
PAGE_SIZE = 128

D_A = D_MODEL // 2
HEAD_A = 64
H_A = D_A // HEAD_A
CHUNK = 128
D_B = D_MODEL - D_A
HEAD_B = 64
H_B = D_B // HEAD_B
LORA_W = 64
LORA_A = 64
LORA_G = 128
D_B_PROJ = 3 * D_B + LORA_W + LORA_A + LORA_G
D_PROJ = 2 * D_A + D_B_PROJ
B_SPLITS = (D_B, 2 * D_B, 3 * D_B, 3 * D_B + LORA_W, 3 * D_B + LORA_W + LORA_A)
DECAY_SCALE = math.exp(-0.5)
GN_EPS = 64e-5
N_EXPERTS = 32
TOP_K = 4
D_FF = D_MODEL
SWIGLU_ALPHA = 1.702
SWIGLU_LIMIT = 7.0
MOE_BLOCK = 128
ALPHA = (2 * DEPTH) ** 0.25
BETA = (8 * DEPTH) ** -0.25
LN_EPS = 1e-5

kernel_name = 'hybrid_gmlp_rwkv7_moe_deepnorm_step'


def layer_norm(x, g, b, eps=LN_EPS):
    xf = x.astype(jnp.float32)
    mu = jnp.mean(xf, axis=-1, keepdims=True)
    var = jnp.mean(jnp.square(xf - mu), axis=-1, keepdims=True)
    return ((xf - mu) * lax.rsqrt(var + eps) * g + b).astype(x.dtype)


def chunk_spatial_gate(u, v, w_s, b_s):
    bn, t, _ = u.shape
    pad = (-t) % CHUNK
    n_chunks = (t + pad) // CHUNK
    v_c = jnp.pad(v, ((0, 0), (0, pad), (0, 0))).reshape(bn, n_chunks, CHUNK, H_A, HEAD_A)
    w_causal = w_s * jnp.tril(jnp.ones((CHUNK, CHUNK), w_s.dtype))
    s = jnp.einsum('hij,bcjhd->bcihd', w_causal, v_c) + b_s.T[:, :, None]
    s = s.reshape(bn, n_chunks * CHUNK, D_A)[:, :t]
    return u * s


def rwkv7_recurrence(r, w, k, v, kk, a_kk, state0):
    def step(S, inp):
        r_t, w_t, k_t, v_t, kk_t, akk_t = inp
        s_kk = jnp.einsum('bhvk,bhk->bhv', S, kk_t)
        S = (S * w_t[:, :, None, :] - s_kk[..., None] * akk_t[:, :, None, :]
             + v_t[..., None] * k_t[:, :, None, :])
        return S, jnp.einsum('bhvk,bhk->bhv', S, r_t)
    xs = tuple(jnp.moveaxis(z, 1, 0) for z in (r, w, k, v, kk, a_kk))
    S, ys = lax.scan(step, state0, xs)
    return jnp.moveaxis(ys, 0, 1), S


def rwkv7_branch(pb, shift0, wkv0, p, l):
    bn, t, _ = pb.shape
    prev = jnp.concatenate([shift0[:, None, :].astype(pb.dtype), pb[:, :-1]], axis=1)
    xs = (pb + p['mu_shift'][l] * (prev - pb)).astype(jnp.float32)
    r, k, v, xw, xa, xg = jnp.split(xs, B_SPLITS, axis=-1)
    w = jnp.exp(-DECAY_SCALE * jax.nn.sigmoid(p['w0'][l] + jnp.tanh(xw) @ p['w_decay_up'][l]))
    a = jax.nn.sigmoid(p['a0'][l] + xa @ p['w_iclr_up'][l])
    g = jax.nn.sigmoid(xg) @ p['w_gate_up'][l]
    heads = lambda z: z.reshape(bn, t, H_B, HEAD_B)
    kk = heads(k * p['k_k'][l])
    kk = kk / jnp.maximum(jnp.linalg.norm(kk, axis=-1, keepdims=True), 1e-12)
    k = k * (1.0 + (a - 1.0) * p['k_a'][l])
    r_h, k_h, v_h = heads(r), heads(k), heads(v)
    y, S = rwkv7_recurrence(r_h, heads(w), k_h, v_h, kk, kk * heads(a), wkv0.astype(jnp.float32))
    y = layer_norm(y, p['ln_x_g'][l].reshape(H_B, HEAD_B), p['ln_x_b'][l].reshape(H_B, HEAD_B), eps=GN_EPS)
    y = y + jnp.sum(r_h * k_h * p['r_k'][l], axis=-1, keepdims=True) * v_h
    return (y.reshape(bn, t, D_B) * g).astype(pb.dtype), pb[:, -1], S.astype(pb.dtype)


def token_mixer(h, shift0, wkv0, p, l):
    proj = h @ p['w_in'][l]
    ua, va = jnp.split(jax.nn.gelu(proj[..., :2 * D_A]), 2, axis=-1)
    va = layer_norm(va, p['ln_v_g'][l], p['ln_v_b'][l])
    ya = chunk_spatial_gate(ua, va, p['w_s'][l], p['b_s'][l])
    yb, shift_new, wkv_new = rwkv7_branch(proj[..., 2 * D_A:], shift0, wkv0, p, l)
    y = jnp.concatenate([ya, yb], axis=-1) @ p['w_out'][l]
    return y, shift_new, wkv_new, va


def moe_ffn(h, p, l):
    bn, t, d = h.shape
    x = h.reshape(bn * t, d)
    n_tok = x.shape[0]
    w_up, b_up, w_down, b_down = p['w_up'][l], p['b_up'][l], p['w_down'][l], p['b_down'][l]
    logits = (x @ p['w_router'][l] + p['b_router'][l]).astype(jnp.float32)
    top_val, top_idx = lax.top_k(logits, TOP_K)
    gates = jax.nn.softmax(top_val, axis=-1).astype(x.dtype)
    n_assign = n_tok * TOP_K
    flat_e = top_idx.reshape(-1)
    order = jnp.argsort(flat_e)
    sorted_e = flat_e[order]
    counts = jnp.bincount(flat_e, length=N_EXPERTS)
    padded = (counts + MOE_BLOCK - 1) // MOE_BLOCK * MOE_BLOCK
    start = jnp.cumsum(counts) - counts
    pstart = jnp.cumsum(padded) - padded
    dest = pstart[sorted_e] + jnp.arange(n_assign) - start[sorted_e]
    n_blocks = -(-n_assign // MOE_BLOCK) + N_EXPERTS
    n_rows = n_blocks * MOE_BLOCK
    row_token = jnp.full((n_rows,), n_tok, jnp.int32).at[dest].set((order // TOP_K).astype(jnp.int32))
    row_gate = jnp.zeros((n_rows,), x.dtype).at[dest].set(gates.reshape(-1)[order])
    block_expert = jnp.minimum(
        jnp.searchsorted(jnp.cumsum(padded), jnp.arange(n_blocks) * MOE_BLOCK, side='right'),
        N_EXPERTS - 1).astype(jnp.int32)
    x_pad = jnp.concatenate([x, jnp.zeros((1, d), x.dtype)], axis=0)
    rows = x_pad[row_token].reshape(n_blocks, MOE_BLOCK, d)

    def run_block(args):
        xb, e = args
        hcat = xb @ w_up[e] + b_up[e]
        glu = jnp.minimum(hcat[:, :D_FF], SWIGLU_LIMIT)
        lin = jnp.clip(hcat[:, D_FF:], -SWIGLU_LIMIT, SWIGLU_LIMIT)
        act = glu * jax.nn.sigmoid(SWIGLU_ALPHA * glu) * (lin + 1.0)
        return act @ w_down[e] + b_down[e]

    y = lax.map(run_block, (rows, block_expert)).reshape(n_rows, d)
    out = jnp.zeros((n_tok + 1, d), x.dtype).at[row_token].add(y * row_gate[:, None])
    return out[:n_tok].reshape(bn, t, d)


def trunk(x, c, shift_state, wkv_state, p):
    x = layer_norm(x, p['ln_in_g'], p['ln_in_b'])
    c_act = jax.nn.silu(c)
    new_shift, new_wkv, new_v = [], [], []
    for l in range(DEPTH):
        mods = (c_act @ p['w_ada'][l] + p['b_ada'][l])[:, None, :]
        sh1, sc1, g1, sh2, sc2, g2 = jnp.split(mods, 6, axis=-1)
        mix, s_l, S_l, v_l = token_mixer(x * (1.0 + sc1) + sh1, shift_state[l], wkv_state[l], p, l)
        x = layer_norm(ALPHA * x + g1 * mix, p['ln1_g'][l], p['ln1_b'][l])
        ffn = moe_ffn(x * (1.0 + sc2) + sh2, p, l)
        x = layer_norm(ALPHA * x + g2 * ffn, p['ln2_g'][l], p['ln2_b'][l])
        new_shift.append(s_l)
        new_wkv.append(S_l)
        new_v.append(v_l)
    return x, jnp.stack(new_shift), jnp.stack(new_wkv), jnp.stack(new_v)


def setup_inputs(seed: int = 0) -> dict:
    key = jax.random.key(seed)
    ks = iter(jax.random.split(key, 48))
    nrm = lambda shape, s: s * jax.random.normal(next(ks), shape, jnp.float32)
    gain = lambda shape: 1.0 + nrm(shape, 0.02)
    return {
        'x_prompt': nrm((BATCH, SEQ, D_MODEL), 1.0),
        'x_sample': nrm((DEC_BATCH, DEC_SEQ, D_MODEL), 1.0),
        'state_wkv': nrm((DEPTH, DEC_BATCH, H_B, HEAD_B, HEAD_B), 0.3),
        'state_shift': nrm((DEPTH, DEC_BATCH, D_B_PROJ), 1.0),
        'c_prompt': nrm((BATCH, D_MODEL), 1.0),
        'c_sample': nrm((DEC_BATCH, D_MODEL), 1.0),
        'ln_in_g': gain((D_MODEL,)),
        'ln_in_b': nrm((D_MODEL,), 0.02),
        'w_ada': nrm((DEPTH, D_MODEL, 6 * D_MODEL), 0.5 * D_MODEL ** -0.5),
        'b_ada': nrm((DEPTH, 6 * D_MODEL), 0.02),
        'w_in': nrm((DEPTH, D_MODEL, D_PROJ), D_MODEL ** -0.5),
        'ln_v_g': gain((DEPTH, D_A)),
        'ln_v_b': nrm((DEPTH, D_A), 0.02),
        'w_s': nrm((DEPTH, H_A, CHUNK, CHUNK), 0.5 * CHUNK ** -0.5),
        'b_s': 1.0 + nrm((DEPTH, H_A, CHUNK), 0.02),
        'mu_shift': jax.random.uniform(next(ks), (DEPTH, D_B_PROJ), jnp.float32),
        'w0': nrm((DEPTH, D_B), 1.0),
        'w_decay_up': nrm((DEPTH, LORA_W, D_B), 0.5 * LORA_W ** -0.5),
        'a0': nrm((DEPTH, D_B), 0.5),
        'w_iclr_up': nrm((DEPTH, LORA_A, D_B), 0.5 * LORA_A ** -0.5),
        'w_gate_up': nrm((DEPTH, LORA_G, D_B), LORA_G ** -0.5),
        'k_k': 0.85 + nrm((DEPTH, D_B), 0.05),
        'k_a': 1.0 + nrm((DEPTH, D_B), 0.05),
        'r_k': nrm((DEPTH, H_B, HEAD_B), 0.3),
        'ln_x_g': gain((DEPTH, D_B)),
        'ln_x_b': nrm((DEPTH, D_B), 0.02),
        'w_out': nrm((DEPTH, D_A + D_B, D_MODEL), BETA * (D_A + D_B) ** -0.5),
        'ln1_g': gain((DEPTH, D_MODEL)),
        'ln1_b': nrm((DEPTH, D_MODEL), 0.02),
        'w_router': nrm((DEPTH, D_MODEL, N_EXPERTS), D_MODEL ** -0.5),
        'b_router': nrm((DEPTH, N_EXPERTS), 0.01),
        'w_up': nrm((DEPTH, N_EXPERTS, D_MODEL, 2 * D_FF), D_MODEL ** -0.5),
        'b_up': nrm((DEPTH, N_EXPERTS, 2 * D_FF), 0.02),
        'w_down': nrm((DEPTH, N_EXPERTS, D_FF, D_MODEL), BETA * D_FF ** -0.5),
        'b_down': nrm((DEPTH, N_EXPERTS, D_MODEL), 0.02),
        'ln2_g': gain((DEPTH, D_MODEL)),
        'ln2_b': nrm((DEPTH, D_MODEL), 0.02),
    }


def reference(x_prompt, x_sample, state_wkv, state_shift, c_prompt, c_sample,
              ln_in_g, ln_in_b, w_ada, b_ada, w_in, ln_v_g, ln_v_b, w_s, b_s,
              mu_shift, w0, w_decay_up, a0, w_iclr_up, w_gate_up, k_k, k_a, r_k,
              ln_x_g, ln_x_b, w_out, ln1_g, ln1_b, w_router, b_router,
              w_up, b_up, w_down, b_down, ln2_g, ln2_b):
    p = dict(ln_in_g=ln_in_g, ln_in_b=ln_in_b, w_ada=w_ada, b_ada=b_ada, w_in=w_in,
             ln_v_g=ln_v_g, ln_v_b=ln_v_b, w_s=w_s, b_s=b_s, mu_shift=mu_shift,
             w0=w0, w_decay_up=w_decay_up, a0=a0, w_iclr_up=w_iclr_up, w_gate_up=w_gate_up,
             k_k=k_k, k_a=k_a, r_k=r_k, ln_x_g=ln_x_g, ln_x_b=ln_x_b, w_out=w_out,
             ln1_g=ln1_g, ln1_b=ln1_b, w_router=w_router, b_router=b_router,
             w_up=w_up, b_up=b_up, w_down=w_down, b_down=b_down, ln2_g=ln2_g, ln2_b=ln2_b)
    bp = x_prompt.shape[0]
    shift_zero = jnp.zeros((DEPTH, bp, D_B_PROJ), x_prompt.dtype)
    wkv_zero = jnp.zeros((DEPTH, bp, H_B, HEAD_B, HEAD_B), x_prompt.dtype)
    y_prompt, shift_p, wkv_p, _ = trunk(x_prompt, c_prompt, shift_zero, wkv_zero, p)
    y_sample, shift_s, wkv_s, v_s = trunk(x_sample, c_sample, state_shift, state_wkv, p)
    return (y_prompt, y_sample, wkv_p, shift_p, wkv_s, shift_s, v_s)
```

```python
import functools
import math

import jax
import jax.numpy as jnp
from jax import lax
from jax.experimental import pallas as pl
from jax.experimental.pallas import tpu as pltpu

F32 = jnp.float32
BF16 = jnp.bfloat16

D_MODEL = 1024
DEPTH = 4
D_A = 512
D_B = 512
HEAD = 64
N_HEADS = 8
CHUNK = 128
LORA_W = 64
LORA_A = 64
LORA_G = 128
D_B_PROJ = 3 * D_B + LORA_W + LORA_A + LORA_G
D_PROJ = 2 * D_A + D_B_PROJ
DECAY_SCALE = math.exp(-0.5)
GN_EPS = 64e-5
LN_EPS = 1e-5
N_EXPERTS = 32
TOP_K = 4
D_FF = D_MODEL
SWIGLU_ALPHA = 1.702
SWIGLU_LIMIT = 7.0
ALPHA = (2 * DEPTH) ** 0.25

LANES = 128
V_HALF = HEAD // 2
VMEM_LIMIT = 56 * 1024 * 1024

ROW_TILE = 256
MOE_TILE = 1024
MOE_ROWS = 128
REC_STEPS = 64


def _cparams(*sem):
    return pltpu.CompilerParams(dimension_semantics=sem, vmem_limit_bytes=VMEM_LIMIT)


def _dot(a, b):
    return jnp.dot(a, b, preferred_element_type=F32)


def _ln_rows(x, g, b, eps):
    mu = jnp.mean(x, axis=-1, keepdims=True)
    xc = x - mu
    var = jnp.mean(xc * xc, axis=-1, keepdims=True)
    return xc * lax.rsqrt(var + eps) * g + b


def _split_dot(x, w_bf16):
    hi = x.astype(BF16)
    lo = (x - hi.astype(F32)).astype(BF16)
    return _dot(hi, w_bf16) + _dot(lo, w_bf16)


def _gelu_tanh(x):
    return 0.5 * x * (1.0 + jnp.tanh(0.7978845608028654 * (x + 0.044715 * (x * x * x))))


def _ln_kernel(x_ref, g_ref, b_ref, o_ref):
    o_ref[...] = _ln_rows(x_ref[...], g_ref[...], b_ref[...], LN_EPS)


def _layer_norm(x2d, g, b):
    n = x2d.shape[0]
    tm = min(n, 512)
    return pl.pallas_call(
        _ln_kernel,
        grid=(n // tm,),
        in_specs=[pl.BlockSpec((tm, D_MODEL), lambda i: (i, 0)),
                  pl.BlockSpec((1, D_MODEL), lambda i: (0, 0)),
                  pl.BlockSpec((1, D_MODEL), lambda i: (0, 0))],
        out_specs=pl.BlockSpec((tm, D_MODEL), lambda i: (i, 0)),
        out_shape=jax.ShapeDtypeStruct((n, D_MODEL), F32),
        compiler_params=_cparams("arbitrary"),
    )(x2d, g.reshape(1, D_MODEL), b.reshape(1, D_MODEL))


def _mods_kernel(c_ref, w_ref, b_ref, o_ref):
    c = c_ref[...]
    ca = c * jax.nn.sigmoid(c)
    o_ref[0] = _dot(ca.astype(BF16), w_ref[0].astype(BF16)) + b_ref[0]


def _mods(c_all, w_ada, b_ada):
    n = c_all.shape[0]
    tn = 1536
    return pl.pallas_call(
        _mods_kernel,
        grid=(DEPTH, 6 * D_MODEL // tn),
        in_specs=[pl.BlockSpec((n, D_MODEL), lambda l, j: (0, 0)),
                  pl.BlockSpec((1, D_MODEL, tn), lambda l, j: (l, 0, j)),
                  pl.BlockSpec((1, 1, tn), lambda l, j: (l, 0, j))],
        out_specs=pl.BlockSpec((1, n, tn), lambda l, j: (l, 0, j)),
        out_shape=jax.ShapeDtypeStruct((DEPTH, n, 6 * D_MODEL), F32),
        compiler_params=_cparams("arbitrary", "arbitrary"),
    )(c_all, w_ada, b_ada.reshape(DEPTH, 1, 6 * D_MODEL))


_V_LNV_G, _V_LNV_B, _V_W0, _V_A0, _V_KK, _V_KA, _V_RK, _V_W00, _V_B0, _V_LNX_G, _V_LNX_B = range(11)
_N_VEC = 16


def _inproj_kernel(*refs, seq_mode):
    if seq_mode:
        (x_ref, sh_ref, sc_ref, win_ref, vec_ref, mu_ref, wwa_ref, wg_ref, bd_ref, ws_ref, bsx_ref,
         ya_ref, g_ref, bv_ref, r_ref, w_ref, k_ref, v_ref, kk_ref, akk_ref, shift_ref,
         carry_ref) = refs
    else:
        (x_ref, sh_ref, sc_ref, win_ref, vec_ref, mu_ref, wwa_ref, wg_ref, bd_ref, prev_ref,
         ya_ref, g_ref, bv_ref, r_ref, w_ref, k_ref, v_ref, kk_ref, akk_ref, shift_ref,
         va_ref) = refs
    tt = x_ref.shape[1]
    vec = vec_ref[...]
    row = lambda i: vec[i:i + 1, :]

    x = x_ref[0]
    h = x * (1.0 + sc_ref[0]) + sh_ref[0]
    proj = _dot(h.astype(BF16), win_ref[...])

    ga = _gelu_tanh(proj[:, :2 * D_A])
    ua = ga[:, :D_A]
    va = _ln_rows(ga[:, D_A:], row(_V_LNV_G), row(_V_LNV_B), LN_EPS)
    if seq_mode:
        head_of_lane = lax.broadcasted_iota(jnp.int32, (1, D_A), 1) // HEAD
        ri = lax.broadcasted_iota(jnp.int32, (CHUNK, CHUNK), 0)
        ci = lax.broadcasted_iota(jnp.int32, (CHUNK, CHUNK), 1)
        w_causal = [jnp.where(ri >= ci, ws_ref[hd], 0.0).astype(BF16) for hd in range(N_HEADS)]
        for c in range(tt // CHUNK):
            rows = slice(c * CHUNK, (c + 1) * CHUNK)
            vc = va[rows, :]
            s = bsx_ref[...]
            for hd in range(N_HEADS):
                v_h = jnp.where(head_of_lane == hd, vc, 0.0).astype(BF16)
                s = s + _dot(w_causal[hd], v_h)
            ya_ref[rows, :] = (ua[rows, :] * s).astype(ya_ref.dtype)
    else:
        va_ref[...] = va
        ya_ref[...] = (ua * (va * row(_V_W00) + row(_V_B0))).astype(ya_ref.dtype)

    pb = proj[:, 2 * D_A:]
    if seq_mode:
        @pl.when(pl.program_id(1) == 0)
        def _():
            carry_ref[...] = jnp.zeros_like(carry_ref)
        rolled = pltpu.roll(pb, 1, 0)
        first = lax.broadcasted_iota(jnp.int32, (tt, 1), 0) == 0
        prev = jnp.where(first, carry_ref[0:1, :], rolled)
        carry_ref[0:1, :] = pb[tt - 1:tt, :]
        shift_ref[0] = pb[tt - 1:tt, :]
    else:
        prev = prev_ref[0]
        shift_ref[0] = pb
    xs = pb + mu_ref[...] * (prev - pb)
    r = xs[:, :D_B]
    k = xs[:, D_B:2 * D_B]
    v = xs[:, 2 * D_B:3 * D_B]
    xwa = xs[:, 3 * D_B:3 * D_B + LORA_W + LORA_A]
    xg = xs[:, 3 * D_B + LORA_W + LORA_A:]
    lane = lax.broadcasted_iota(jnp.int32, (1, LORA_W + LORA_A), 1)
    lora_in = jnp.where(lane < LORA_W, jnp.tanh(xwa), xwa)
    wa = _dot(lora_in.astype(BF16), wwa_ref[...])
    w = jnp.exp(-DECAY_SCALE * jax.nn.sigmoid(row(_V_W0) + wa[:, :D_B]))
    a = jax.nn.sigmoid(row(_V_A0) + wa[:, D_B:])
    g = _dot(jax.nn.sigmoid(xg).astype(BF16), wg_ref[...])
    bd = bd_ref[...]
    kk = k * row(_V_KK)
    kk = kk * lax.rsqrt(jnp.maximum(_split_dot(kk * kk, bd), 1e-24))
    k2 = k * (1.0 + (a - 1.0) * row(_V_KA))
    bonus = _split_dot(r * k2 * row(_V_RK), bd)
    g_ref[...] = g
    bv_ref[...] = bonus * v
    r_ref[...] = r
    w_ref[...] = w
    k_ref[...] = k2
    v_ref[...] = v
    kk_ref[...] = kk
    akk_ref[...] = kk * a


def _inproj(x3, mods3, lw, seq_mode, prev3=None):
    bk, tk, _ = x3.shape
    rm = mods3.shape[1]
    tt = min(tk, ROW_TILE)
    n = bk * tk
    rmb = 1 if rm == 1 else tt
    mod_spec = lambda j: pl.BlockSpec((1, rmb, D_MODEL),
                                      (lambda b, t: (b, 0, j)) if rm == 1 else (lambda b, t: (b, t, j)))
    full = lambda shape: pl.BlockSpec(shape, lambda b, t: (0,) * len(shape))
    in_specs = [pl.BlockSpec((1, tt, D_MODEL), lambda b, t: (b, t, 0)),
                mod_spec(0), mod_spec(1),
                full((D_MODEL, D_PROJ)), full((_N_VEC, D_B)), full((1, D_B_PROJ)),
                full((LORA_W + LORA_A, 2 * D_B)), full((LORA_G, D_B)), full((D_B, D_B))]
    args = [x3, mods3, mods3, lw["w_in"], lw["vec"], lw["mu"], lw["wwa"], lw["wg"], lw["bd"]]
    row_spec = pl.BlockSpec((tt, D_B), lambda b, t: (b * (tk // tt) + t, 0))
    row_shape = jax.ShapeDtypeStruct((n, D_B), F32)
    out_specs = [row_spec] * 9
    out_shape = [jax.ShapeDtypeStruct((n, D_A), BF16)] + [row_shape] * 8
    if seq_mode:
        in_specs += [full((N_HEADS, CHUNK, CHUNK)), full((CHUNK, D_A))]
        args += [lw["w_s"], lw["bsx"]]
        out_specs += [pl.BlockSpec((1, 1, D_B_PROJ), lambda b, t: (b, 0, 0))]
        out_shape += [jax.ShapeDtypeStruct((bk, 1, D_B_PROJ), F32)]
        scratch = [pltpu.VMEM((8, D_B_PROJ), F32)]
    else:
        in_specs += [pl.BlockSpec((1, tt, D_B_PROJ), lambda b, t: (b, t, 0))]
        args += [prev3]
        out_specs += [pl.BlockSpec((1, tt, D_B_PROJ), lambda b, t: (b, t, 0)),
                      pl.BlockSpec((tt, D_A), lambda b, t: (b * (tk // tt) + t, 0))]
        out_shape += [jax.ShapeDtypeStruct((bk, tk, D_B_PROJ), F32), row_shape]
        scratch = []
    return pl.pallas_call(
        functools.partial(_inproj_kernel, seq_mode=seq_mode),
        grid=(bk, tk // tt),
        in_specs=in_specs, out_specs=out_specs, out_shape=out_shape,
        scratch_shapes=scratch,
        compiler_params=_cparams("arbitrary", "arbitrary"),
    )(*args)


def _rec_kernel(kk_ref, w_ref, akk_ref, k_ref, r_ref, v_ref, s0_ref, y_ref, st_ref, s_scr):
    tc = pl.program_id(1)
    steps = kk_ref.shape[0]

    @pl.when(tc == 0)
    def _():
        s_scr[...] = s0_ref[...]

    def step(t, carry):
        for vi in range(V_HALF):
            s_old = s_scr[vi]
            skk = jnp.sum(s_old * kk_ref[t], axis=0, keepdims=True)
            v_row = v_ref[t, vi:vi + 1, :]
            s_new = s_old * w_ref[t] + (v_row * k_ref[t] - skk * akk_ref[t])
            s_scr[vi] = s_new
            y_ref[t, vi:vi + 1, :] = jnp.sum(s_new * r_ref[t], axis=0, keepdims=True)
        return carry

    lax.fori_loop(0, steps, step, 0)

    @pl.when(tc == pl.num_programs(1) - 1)
    def _():
        st_ref[...] = s_scr[...]


def _recurrence(kk, w, akk, k, r, v, s0):
    t, _, l = kk.shape
    tc = min(t, REC_STEPS)
    kspec = pl.BlockSpec((tc, HEAD, LANES), lambda g, i: (i, 0, g))
    vspec = pl.BlockSpec((tc, V_HALF, LANES), lambda g, i: (i, 0, g))
    sspec = pl.BlockSpec((V_HALF, HEAD, LANES), lambda g, i: (0, 0, g))
    return pl.pallas_call(
        _rec_kernel,
        grid=(l // LANES, t // tc),
        in_specs=[kspec] * 5 + [vspec, sspec],
        out_specs=[vspec, sspec],
        out_shape=[jax.ShapeDtypeStruct((t, V_HALF, l), F32),
                   jax.ShapeDtypeStruct((V_HALF, HEAD, l), F32)],
        scratch_shapes=[pltpu.VMEM((V_HALF, HEAD, LANES), F32)],
        compiler_params=_cparams("arbitrary", "arbitrary"),
    )(kk, w, akk, k, r, v, s0)


def _n_groups(bk):
    return (bk * N_HEADS) // (LANES // 2)


def _to_chain_k(x2d, bk, tk):
    g = _n_groups(bk)
    x = x2d.reshape(bk, tk, N_HEADS, HEAD).transpose(1, 3, 0, 2).reshape(tk, HEAD, g, 1, LANES // 2)
    return jnp.broadcast_to(x, (tk, HEAD, g, 2, LANES // 2)).reshape(tk, HEAD, g * LANES)


def _to_chain_v(x2d, bk, tk):
    g = _n_groups(bk)
    x = x2d.reshape(bk, tk, N_HEADS, 2, V_HALF).transpose(1, 4, 3, 0, 2)
    x = x.reshape(tk, V_HALF, 2, g, LANES // 2).transpose(0, 1, 3, 2, 4)
    return x.reshape(tk, V_HALF, g * LANES)


def _from_chain_v(y, bk, tk):
    g = _n_groups(bk)
    y = y.reshape(tk, V_HALF, g, 2, LANES // 2).transpose(0, 1, 3, 2, 4)
    y = y.reshape(tk, V_HALF, 2, bk, N_HEADS).transpose(3, 0, 4, 2, 1)
    return y.reshape(bk * tk, D_B)


def _state_to_chain(s, bk):
    g = _n_groups(bk)
    s = s.reshape(g, LANES // 2, 2, V_HALF, HEAD).transpose(3, 4, 0, 2, 1)
    return s.reshape(V_HALF, HEAD, g * LANES)


def _state_from_chain(sc, bk):
    g = _n_groups(bk)
    s = sc.reshape(V_HALF, HEAD, g, 2, LANES // 2).transpose(2, 4, 3, 0, 1)
    return s.reshape(bk, N_HEADS, HEAD, HEAD)


def _outproj_kernel(y_ref, bv_ref, g_ref, ya_ref, x_ref, g1_ref, sh2_ref, sc2_ref, vec_ref, lnd_ref,
                    wout_ref, wrh_ref, wrl_ref, br_ref, bd_ref, x1_ref, h2_ref, gd_ref):
    vec = vec_ref[...]
    bd = bd_ref[...]
    y = y_ref[...]
    mean = _split_dot(y, bd) * (1.0 / HEAD)
    yc = y - mean
    var = _split_dot(yc * yc, bd) * (1.0 / HEAD)
    yn = yc * lax.rsqrt(var + GN_EPS) * vec[_V_LNX_G:_V_LNX_G + 1] + vec[_V_LNX_B:_V_LNX_B + 1]
    yb = (yn + bv_ref[...]) * g_ref[...]
    mix = _dot(ya_ref[...], wout_ref[:D_A, :]) + _dot(yb.astype(BF16), wout_ref[D_A:, :])
    lnd = lnd_ref[...]
    x1 = _ln_rows(ALPHA * x_ref[0] + g1_ref[0] * mix, lnd[0:1], lnd[1:2], LN_EPS)
    x1_ref[0] = x1
    h2 = x1 * (1.0 + sc2_ref[0]) + sh2_ref[0]
    h2_ref[0] = h2.astype(h2_ref.dtype)

    hi = h2.astype(BF16)
    lo = (h2 - hi.astype(F32)).astype(BF16)
    logits = _dot(hi, wrh_ref[...]) + _dot(hi, wrl_ref[...]) + _dot(lo, wrh_ref[...]) + br_ref[...]
    lane = lax.broadcasted_iota(jnp.int32, logits.shape, 1).astype(F32)
    work = logits
    tops, picks = [], []
    for _ in range(TOP_K):
        m = jnp.max(work, axis=1, keepdims=True)
        idx = jnp.min(jnp.where(work == m, lane, float(N_EXPERTS)), axis=1, keepdims=True)
        pick = lane == idx
        tops.append(m)
        picks.append(pick)
        work = jnp.where(pick, -jnp.inf, work)
    exps = [jnp.exp(m - tops[0]) for m in tops]
    den = exps[0] + exps[1] + exps[2] + exps[3]
    gd = jnp.full(logits.shape, -1.0, F32)
    for pick, e in zip(picks, exps):
        gd = jnp.where(pick, e / den, gd)
    gd_ref[0] = gd


def _outproj(y, bv, g, ya, x3, mods3, lw):
    bk, tk, _ = x3.shape
    rm = mods3.shape[1]
    tt = min(tk, ROW_TILE)
    n = bk * tk
    rmb = 1 if rm == 1 else tt
    mod_spec = lambda j: pl.BlockSpec((1, rmb, D_MODEL),
                                      (lambda b, t: (b, 0, j)) if rm == 1 else (lambda b, t: (b, t, j)))
    full = lambda shape: pl.BlockSpec(shape, lambda b, t: (0,) * len(shape))
    row = lambda width: pl.BlockSpec((tt, width), lambda b, t: (b * (tk // tt) + t, 0))
    tok = lambda width: pl.BlockSpec((1, tt, width), lambda b, t: (b, t, 0))
    return pl.pallas_call(
        _outproj_kernel,
        grid=(bk, tk // tt),
        in_specs=[row(D_B), row(D_B), row(D_B), row(D_A), tok(D_MODEL),
                  mod_spec(2), mod_spec(3), mod_spec(4),
                  full((_N_VEC, D_B)), full((2, D_MODEL)), full((D_MODEL, D_MODEL)),
                  full((D_MODEL, N_EXPERTS)), full((D_MODEL, N_EXPERTS)), full((1, N_EXPERTS)),
                  full((D_B, D_B))],
        out_specs=[tok(D_MODEL), tok(D_MODEL), tok(N_EXPERTS)],
        out_shape=[jax.ShapeDtypeStruct((bk, tk, D_MODEL), F32),
                   jax.ShapeDtypeStruct((bk, tk, D_MODEL), BF16),
                   jax.ShapeDtypeStruct((bk, tk, N_EXPERTS), F32)],
        compiler_params=_cparams("arbitrary", "arbitrary"),
    )(y, bv, g, ya, x3, mods3, mods3, mods3, lw["vec"], lw["ln1"], lw["w_out"],
      lw["wr_hi"], lw["wr_lo"], lw["b_router"], lw["bd"])


def _moe_kernel(h_ref, gdt_ref, u_ref, wup_ref, bup_ref, wdn_ref, bdn_ref, o_ref, rank_scr):
    e = pl.program_id(1)
    tm = h_ref.shape[0]

    @pl.when(e == 0)
    def _():
        o_ref[...] = jnp.zeros_like(o_ref)
        sel = jnp.where(gdt_ref[...] >= 0.0, 1.0, 0.0).astype(BF16)
        rank_scr[...] = _dot(sel, u_ref[...])

    gd_row = gdt_ref[pl.ds(e, 1), :]
    sel_row = gd_row >= 0.0
    count = jnp.sum(jnp.where(sel_row, 1, 0).astype(jnp.int32))
    n_blocks = (count + MOE_ROWS - 1) // MOE_ROWS
    gate_row = jnp.where(sel_row, gd_row, 0.0)
    slot_row = jnp.where(sel_row, rank_scr[pl.ds(e, 1), :].astype(jnp.int32), -1)
    row_id = lax.broadcasted_iota(jnp.int32, (MOE_ROWS, tm), 0)

    def block(b, carry):
        onehot = jnp.where(slot_row == row_id + b * MOE_ROWS, 1.0, 0.0)
        p = onehot.astype(BF16)
        xg = _dot(p, h_ref[...])
        hcat = _dot(xg.astype(BF16), wup_ref[0]) + bup_ref[0]
        glu = jnp.minimum(hcat[:, :D_FF], SWIGLU_LIMIT)
        lin = jnp.clip(hcat[:, D_FF:], -SWIGLU_LIMIT, SWIGLU_LIMIT)
        act = glu * jax.nn.sigmoid(SWIGLU_ALPHA * glu) * (lin + 1.0)
        yb = _dot(act.astype(BF16), wdn_ref[0]) + bdn_ref[0]
        gate = jnp.sum(onehot * gate_row, axis=1, keepdims=True)
        ybg = (yb * gate).astype(BF16)
        o_ref[...] += lax.dot_general(p, ybg, (((0,), (0,)), ((), ())), preferred_element_type=F32)
        return carry

    lax.fori_loop(0, n_blocks, block, 0)


def _moe(h2, gdt, lw):
    n = h2.shape[0]
    tm = min(n, MOE_TILE)
    return pl.pallas_call(
        _moe_kernel,
        grid=(n // tm, N_EXPERTS),
        in_specs=[pl.BlockSpec((tm, D_MODEL), lambda i, e: (i, 0)),
                  pl.BlockSpec((N_EXPERTS, tm), lambda i, e: (0, i)),
                  pl.BlockSpec((tm, tm), lambda i, e: (0, 0)),
                  pl.BlockSpec((1, D_MODEL, 2 * D_FF), lambda i, e: (e, 0, 0)),
                  pl.BlockSpec((1, 1, 2 * D_FF), lambda i, e: (e, 0, 0)),
                  pl.BlockSpec((1, D_FF, D_MODEL), lambda i, e: (e, 0, 0)),
                  pl.BlockSpec((1, 1, D_MODEL), lambda i, e: (e, 0, 0))],
        out_specs=pl.BlockSpec((tm, D_MODEL), lambda i, e: (i, 0)),
        out_shape=jax.ShapeDtypeStruct((n, D_MODEL), F32),
        scratch_shapes=[pltpu.VMEM((N_EXPERTS, tm), F32)],
        compiler_params=_cparams("arbitrary", "arbitrary"),
    )(h2, gdt, lw["tri"][:tm, :tm], lw["w_up"], lw["b_up"], lw["w_down"], lw["b_down"])


def _ln2_kernel(x_ref, f_ref, g2_ref, lnd_ref, o_ref):
    lnd = lnd_ref[...]
    o_ref[0] = _ln_rows(ALPHA * x_ref[0] + g2_ref[0] * f_ref[0], lnd[0:1], lnd[1:2], LN_EPS)


def _ln2(x3, f3, mods3, lw):
    bk, tk, _ = x3.shape
    rm = mods3.shape[1]
    tt = min(tk, 512)
    rmb = 1 if rm == 1 else tt
    tok = pl.BlockSpec((1, tt, D_MODEL), lambda b, t: (b, t, 0))
    return pl.pallas_call(
        _ln2_kernel,
        grid=(bk, tk // tt),
        in_specs=[tok, tok,
                  pl.BlockSpec((1, rmb, D_MODEL),
                               (lambda b, t: (b, 0, 5)) if rm == 1 else (lambda b, t: (b, t, 5))),
                  pl.BlockSpec((2, D_MODEL), lambda b, t: (0, 0))],
        out_specs=tok,
        out_shape=jax.ShapeDtypeStruct((bk, tk, D_MODEL), F32),
        compiler_params=_cparams("arbitrary", "arbitrary"),
    )(x3, f3, mods3, lw["ln2"])


def _mixer(x3, mods3, lw, seq_mode, shift3, wkv):
    bk, tk, _ = x3.shape
    if seq_mode:
        ya, g, bv, r, w, k2, v, kk, akk, shift_new = _inproj(x3, mods3, lw, True)
        nb, nt, va = bk, tk, None
        s0 = jnp.zeros((V_HALF, HEAD, _n_groups(nb) * LANES), F32)
    else:
        ya, g, bv, r, w, k2, v, kk, akk, shift_new, va = _inproj(x3, mods3, lw, False, shift3)
        nb, nt = tk, 1
        s0 = _state_to_chain(wkv, nb)
    ck = lambda z: _to_chain_k(z, nb, nt)
    y, s_t = _recurrence(ck(kk), ck(w), ck(akk), ck(k2), ck(r), _to_chain_v(v, nb, nt), s0)
    y = _from_chain_v(y, nb, nt)
    x1, h2, gd = _outproj(y, bv, g, ya, x3, mods3, lw)
    return x1, h2, gd, shift_new, _state_from_chain(s_t, nb), va


def kernel(x_prompt, x_sample, state_wkv, state_shift, c_prompt, c_sample, ln_in_g, ln_in_b, w_ada, b_ada, w_in, ln_v_g, ln_v_b, w_s, b_s, mu_shift, w0, w_decay_up, a0, w_iclr_up, w_gate_up, k_k, k_a, r_k, ln_x_g, ln_x_b, w_out, ln1_g, ln1_b, w_router, b_router, w_up, b_up, w_down, b_down, ln2_g, ln2_b):
    bp, tp, _ = x_prompt.shape
    bs = x_sample.shape[0]
    n_p, n_s = bp * tp, bs
    n_all = -(-(n_p + n_s) // MOE_TILE) * MOE_TILE

    lane_head = jnp.arange(D_B) // HEAD
    bd = (lane_head[:, None] == lane_head[None, :]).astype(BF16)
    tri = (jnp.arange(MOE_TILE)[:, None] < jnp.arange(MOE_TILE)[None, :]).astype(BF16)
    zeros_lora = jnp.zeros((LORA_W, D_B), F32)
    layers = []
    for l in range(DEPTH):
        rows = [ln_v_g[l], ln_v_b[l], w0[l], a0[l], k_k[l], k_a[l], r_k[l].reshape(D_B),
                jnp.repeat(w_s[l, :, 0, 0], HEAD), jnp.repeat(b_s[l, :, 0], HEAD), ln_x_g[l], ln_x_b[l]]
        vec = jnp.stack(rows + [jnp.zeros((D_B,), F32)] * (_N_VEC - len(rows)))
        wr_hi = w_router[l].astype(BF16)
        layers.append(dict(
            w_in=w_in[l].astype(BF16), vec=vec, mu=mu_shift[l].reshape(1, D_B_PROJ),
            wwa=jnp.concatenate([jnp.concatenate([w_decay_up[l], zeros_lora], axis=1),
                                 jnp.concatenate([zeros_lora, w_iclr_up[l]], axis=1)], axis=0).astype(BF16),
            wg=w_gate_up[l].astype(BF16), bd=bd, w_s=w_s[l],
            bsx=jnp.repeat(b_s[l].T, HEAD, axis=1),
            w_out=w_out[l].astype(BF16),
            ln1=jnp.stack([ln1_g[l], ln1_b[l]]), ln2=jnp.stack([ln2_g[l], ln2_b[l]]),
            wr_hi=wr_hi, wr_lo=(w_router[l] - wr_hi.astype(F32)).astype(BF16),
            b_router=b_router[l].reshape(1, N_EXPERTS), tri=tri,
            w_up=w_up[l].astype(BF16), b_up=b_up[l].reshape(N_EXPERTS, 1, 2 * D_FF),
            w_down=w_down[l].astype(BF16), b_down=b_down[l].reshape(N_EXPERTS, 1, D_MODEL)))

    mods = _mods(jnp.concatenate([c_prompt, c_sample], axis=0), w_ada, b_ada)
    xp = _layer_norm(x_prompt.reshape(n_p, D_MODEL), ln_in_g, ln_in_b).reshape(bp, tp, D_MODEL)
    xs = _layer_norm(x_sample.reshape(n_s, D_MODEL), ln_in_g, ln_in_b).reshape(1, bs, D_MODEL)

    wkv_p, shift_p, wkv_s, shift_s, v_s = [], [], [], [], []
    for l in range(DEPTH):
        lw = layers[l]
        mods_p = mods[l, :bp].reshape(bp, 1, 6 * D_MODEL)
        mods_s = mods[l, bp:].reshape(1, bs, 6 * D_MODEL)
        x1p, h2p, gdp, sh_p, s_p, _ = _mixer(xp, mods_p, lw, True, None, None)
        x1s, h2s, gds, sh_s, s_s, va_s = _mixer(xs, mods_s, lw, False,
                                                state_shift[l].reshape(1, bs, D_B_PROJ), state_wkv[l])
        pad = n_all - n_p - n_s
        h2 = jnp.concatenate([h2p.reshape(n_p, D_MODEL), h2s.reshape(n_s, D_MODEL),
                              jnp.zeros((pad, D_MODEL), BF16)], axis=0)
        gd = jnp.concatenate([gdp.reshape(n_p, N_EXPERTS), gds.reshape(n_s, N_EXPERTS),
                              jnp.full((pad, N_EXPERTS), -1.0, F32)], axis=0)
        ffn = _moe(h2, gd.T, lw)
        xp = _ln2(x1p, ffn[:n_p].reshape(bp, tp, D_MODEL), mods_p, lw)
        xs = _ln2(x1s, ffn[n_p:n_p + n_s].reshape(1, bs, D_MODEL), mods_s, lw)
        wkv_p.append(s_p)
        shift_p.append(sh_p.reshape(bp, D_B_PROJ))
        wkv_s.append(s_s)
        shift_s.append(sh_s.reshape(bs, D_B_PROJ))
        v_s.append(va_s.reshape(bs, 1, D_A))

    return (xp, xs.reshape(bs, 1, D_MODEL), jnp.stack(wkv_p), jnp.stack(shift_p),
            jnp.stack(wkv_s), jnp.stack(shift_s), jnp.stack(v_s))
```

```python
import functools
import math

import jax
import jax.numpy as jnp
from jax import lax
from jax.experimental import pallas as pl
from jax.experimental.pallas import tpu as pltpu

F32 = jnp.float32
BF16 = jnp.bfloat16

D_MODEL = 1024
DEPTH = 4
D_A = 512
D_B = 512
HEAD = 64
N_HEADS = 8
CHUNK = 128
LORA_W = 64
LORA_A = 64
LORA_G = 128
D_B_PROJ = 3 * D_B + LORA_W + LORA_A + LORA_G
D_PROJ = 2 * D_A + D_B_PROJ
DECAY_SCALE = math.exp(-0.5)
GN_EPS = 64e-5
LN_EPS = 1e-5
N_EXPERTS = 32
TOP_K = 4
D_FF = D_MODEL
SWIGLU_ALPHA = 1.702
SWIGLU_LIMIT = 7.0
ALPHA = (2 * DEPTH) ** 0.25

LANES = 128
V_HALF = HEAD // 2
VMEM_LIMIT = 56 * 1024 * 1024

ROW_TILE = 256
MOE_TILE = 1024
MOE_ROWS = 144
MOE_FLUSH = 8
REC_STEPS = 64


def _cparams(*sem):
    return pltpu.CompilerParams(dimension_semantics=sem, vmem_limit_bytes=VMEM_LIMIT)


def _dot(a, b):
    return jnp.dot(a, b, preferred_element_type=F32)


def _ln_rows(x, g, b, eps):
    mu = jnp.mean(x, axis=-1, keepdims=True)
    xc = x - mu
    var = jnp.mean(xc * xc, axis=-1, keepdims=True)
    return xc * lax.rsqrt(var + eps) * g + b


def _split_dot(x, w_bf16):
    hi = x.astype(BF16)
    lo = (x - hi.astype(F32)).astype(BF16)
    return _dot(hi, w_bf16) + _dot(lo, w_bf16)


def _gelu_tanh(x):
    return 0.5 * x * (1.0 + jnp.tanh(0.7978845608028654 * (x + 0.044715 * (x * x * x))))


def _ln_kernel(x_ref, g_ref, b_ref, o_ref):
    o_ref[...] = _ln_rows(x_ref[...], g_ref[...], b_ref[...], LN_EPS)


def _layer_norm(x2d, g, b):
    n = x2d.shape[0]
    tm = min(n, 512)
    return pl.pallas_call(
        _ln_kernel,
        grid=(n // tm,),
        in_specs=[pl.BlockSpec((tm, D_MODEL), lambda i: (i, 0)),
                  pl.BlockSpec((1, D_MODEL), lambda i: (0, 0)),
                  pl.BlockSpec((1, D_MODEL), lambda i: (0, 0))],
        out_specs=pl.BlockSpec((tm, D_MODEL), lambda i: (i, 0)),
        out_shape=jax.ShapeDtypeStruct((n, D_MODEL), F32),
        compiler_params=_cparams("arbitrary"),
    )(x2d, g.reshape(1, D_MODEL), b.reshape(1, D_MODEL))


def _mods_kernel(c_ref, w_ref, b_ref, o_ref):
    c = c_ref[...]
    ca = c * jax.nn.sigmoid(c)
    o_ref[0] = _dot(ca.astype(BF16), w_ref[0].astype(BF16)) + b_ref[0]


def _mods(c_all, w_ada, b_ada):
    n = c_all.shape[0]
    tn = 1536
    return pl.pallas_call(
        _mods_kernel,
        grid=(DEPTH, 6 * D_MODEL // tn),
        in_specs=[pl.BlockSpec((n, D_MODEL), lambda l, j: (0, 0)),
                  pl.BlockSpec((1, D_MODEL, tn), lambda l, j: (l, 0, j)),
                  pl.BlockSpec((1, 1, tn), lambda l, j: (l, 0, j))],
        out_specs=pl.BlockSpec((1, n, tn), lambda l, j: (l, 0, j)),
        out_shape=jax.ShapeDtypeStruct((DEPTH, n, 6 * D_MODEL), F32),
        compiler_params=_cparams("arbitrary", "arbitrary"),
    )(c_all, w_ada, b_ada.reshape(DEPTH, 1, 6 * D_MODEL))


_V_LNV_G, _V_LNV_B, _V_W0, _V_A0, _V_KK, _V_KA, _V_RK, _V_W00, _V_B0, _V_LNX_G, _V_LNX_B = range(11)
_N_VEC = 16


def _inproj_kernel(*refs, seq_mode):
    if seq_mode:
        (x_ref, sh_ref, sc_ref, win_ref, vec_ref, mu_ref, wwa_ref, wg_ref, bd_ref, ws_ref, bsx_ref,
         ya_ref, g_ref, bv_ref, r_ref, w_ref, k_ref, v_ref, kk_ref, akk_ref, shift_ref,
         carry_ref) = refs
    else:
        (x_ref, sh_ref, sc_ref, win_ref, vec_ref, mu_ref, wwa_ref, wg_ref, bd_ref, prev_ref,
         ya_ref, g_ref, bv_ref, r_ref, w_ref, k_ref, v_ref, kk_ref, akk_ref, shift_ref,
         va_ref) = refs
    tt = x_ref.shape[1]
    vec = vec_ref[...]
    row = lambda i: vec[i:i + 1, :]

    x = x_ref[0]
    h = x * (1.0 + sc_ref[0]) + sh_ref[0]
    proj = _dot(h.astype(BF16), win_ref[...])

    ga = _gelu_tanh(proj[:, :2 * D_A])
    ua = ga[:, :D_A]
    va = _ln_rows(ga[:, D_A:], row(_V_LNV_G), row(_V_LNV_B), LN_EPS)
    if seq_mode:
        head_of_lane = lax.broadcasted_iota(jnp.int32, (1, D_A), 1) // HEAD
        ri = lax.broadcasted_iota(jnp.int32, (CHUNK, CHUNK), 0)
        ci = lax.broadcasted_iota(jnp.int32, (CHUNK, CHUNK), 1)
        w_causal = [jnp.where(ri >= ci, ws_ref[hd], 0.0).astype(BF16) for hd in range(N_HEADS)]
        for c in range(tt // CHUNK):
            rows = slice(c * CHUNK, (c + 1) * CHUNK)
            vc = va[rows, :]
            s = bsx_ref[...]
            for hd in range(N_HEADS):
                v_h = jnp.where(head_of_lane == hd, vc, 0.0).astype(BF16)
                s = s + _dot(w_causal[hd], v_h)
            ya_ref[rows, :] = (ua[rows, :] * s).astype(ya_ref.dtype)
    else:
        va_ref[...] = va
        ya_ref[...] = (ua * (va * row(_V_W00) + row(_V_B0))).astype(ya_ref.dtype)

    pb = proj[:, 2 * D_A:]
    if seq_mode:
        @pl.when(pl.program_id(1) == 0)
        def _():
            carry_ref[...] = jnp.zeros_like(carry_ref)
        rolled = pltpu.roll(pb, 1, 0)
        first = lax.broadcasted_iota(jnp.int32, (tt, 1), 0) == 0
        prev = jnp.where(first, carry_ref[0:1, :], rolled)
        carry_ref[0:1, :] = pb[tt - 1:tt, :]
        shift_ref[0] = pb[tt - 1:tt, :]
    else:
        prev = prev_ref[0]
        shift_ref[0] = pb
    xs = pb + mu_ref[...] * (prev - pb)
    r = xs[:, :D_B]
    k = xs[:, D_B:2 * D_B]
    v = xs[:, 2 * D_B:3 * D_B]
    xwa = xs[:, 3 * D_B:3 * D_B + LORA_W + LORA_A]
    xg = xs[:, 3 * D_B + LORA_W + LORA_A:]
    lane = lax.broadcasted_iota(jnp.int32, (1, LORA_W + LORA_A), 1)
    lora_in = jnp.where(lane < LORA_W, jnp.tanh(xwa), xwa)
    wa = _dot(lora_in.astype(BF16), wwa_ref[...])
    w = jnp.exp(-DECAY_SCALE * jax.nn.sigmoid(row(_V_W0) + wa[:, :D_B]))
    a = jax.nn.sigmoid(row(_V_A0) + wa[:, D_B:])
    g = _dot(jax.nn.sigmoid(xg).astype(BF16), wg_ref[...])
    bd = bd_ref[...]
    kk = k * row(_V_KK)
    kk = kk * lax.rsqrt(jnp.maximum(_split_dot(kk * kk, bd), 1e-24))
    k2 = k * (1.0 + (a - 1.0) * row(_V_KA))
    bonus = _split_dot(r * k2 * row(_V_RK), bd)
    g_ref[...] = g
    bv_ref[...] = bonus * v
    r_ref[...] = r
    w_ref[...] = w
    k_ref[...] = k2
    v_ref[...] = v
    kk_ref[...] = kk
    akk_ref[...] = kk * a


def _inproj(x3, mods3, lw, seq_mode, prev3=None):
    bk, tk, _ = x3.shape
    rm = mods3.shape[1]
    tt = min(tk, ROW_TILE)
    n = bk * tk
    rmb = 1 if rm == 1 else tt
    mod_spec = lambda j: pl.BlockSpec((1, rmb, D_MODEL),
                                      (lambda b, t: (b, 0, j)) if rm == 1 else (lambda b, t: (b, t, j)))
    full = lambda shape: pl.BlockSpec(shape, lambda b, t: (0,) * len(shape))
    in_specs = [pl.BlockSpec((1, tt, D_MODEL), lambda b, t: (b, t, 0)),
                mod_spec(0), mod_spec(1),
                full((D_MODEL, D_PROJ)), full((_N_VEC, D_B)), full((1, D_B_PROJ)),
                full((LORA_W + LORA_A, 2 * D_B)), full((LORA_G, D_B)), full((D_B, D_B))]
    args = [x3, mods3, mods3, lw["w_in"], lw["vec"], lw["mu"], lw["wwa"], lw["wg"], lw["bd"]]
    row_spec = pl.BlockSpec((tt, D_B), lambda b, t: (b * (tk // tt) + t, 0))
    row_shape = jax.ShapeDtypeStruct((n, D_B), F32)
    out_specs = [row_spec] * 9
    out_shape = [jax.ShapeDtypeStruct((n, D_A), BF16)] + [row_shape] * 8
    if seq_mode:
        in_specs += [full((N_HEADS, CHUNK, CHUNK)), full((CHUNK, D_A))]
        args += [lw["w_s"], lw["bsx"]]
        out_specs += [pl.BlockSpec((1, 1, D_B_PROJ), lambda b, t: (b, 0, 0))]
        out_shape += [jax.ShapeDtypeStruct((bk, 1, D_B_PROJ), F32)]
        scratch = [pltpu.VMEM((8, D_B_PROJ), F32)]
    else:
        in_specs += [pl.BlockSpec((1, tt, D_B_PROJ), lambda b, t: (b, t, 0))]
        args += [prev3]
        out_specs += [pl.BlockSpec((1, tt, D_B_PROJ), lambda b, t: (b, t, 0)),
                      pl.BlockSpec((tt, D_A), lambda b, t: (b * (tk // tt) + t, 0))]
        out_shape += [jax.ShapeDtypeStruct((bk, tk, D_B_PROJ), F32), row_shape]
        scratch = []
    return pl.pallas_call(
        functools.partial(_inproj_kernel, seq_mode=seq_mode),
        grid=(bk, tk // tt),
        in_specs=in_specs, out_specs=out_specs, out_shape=out_shape,
        scratch_shapes=scratch,
        compiler_params=_cparams("arbitrary", "arbitrary"),
    )(*args)


def _rec_kernel(kk_ref, w_ref, akk_ref, k_ref, r_ref, v_ref, s0_ref, y_ref, st_ref, s_scr):
    tc = pl.program_id(1)
    steps = kk_ref.shape[0]

    @pl.when(tc == 0)
    def _():
        s_scr[...] = s0_ref[...]

    def step(t, carry):
        for vi in range(V_HALF):
            s_old = s_scr[vi]
            skk = jnp.sum(s_old * kk_ref[t], axis=0, keepdims=True)
            v_row = v_ref[t, vi:vi + 1, :]
            s_new = s_old * w_ref[t] + (v_row * k_ref[t] - skk * akk_ref[t])
            s_scr[vi] = s_new
            y_ref[t, vi:vi + 1, :] = jnp.sum(s_new * r_ref[t], axis=0, keepdims=True)
        return carry

    lax.fori_loop(0, steps, step, 0)

    @pl.when(tc == pl.num_programs(1) - 1)
    def _():
        st_ref[...] = s_scr[...]


def _recurrence(kk, w, akk, k, r, v, s0):
    t, _, l = kk.shape
    tc = min(t, REC_STEPS)
    kspec = pl.BlockSpec((tc, HEAD, LANES), lambda g, i: (i, 0, g))
    vspec = pl.BlockSpec((tc, V_HALF, LANES), lambda g, i: (i, 0, g))
    sspec = pl.BlockSpec((V_HALF, HEAD, LANES), lambda g, i: (0, 0, g))
    return pl.pallas_call(
        _rec_kernel,
        grid=(l // LANES, t // tc),
        in_specs=[kspec] * 5 + [vspec, sspec],
        out_specs=[vspec, sspec],
        out_shape=[jax.ShapeDtypeStruct((t, V_HALF, l), F32),
                   jax.ShapeDtypeStruct((V_HALF, HEAD, l), F32)],
        scratch_shapes=[pltpu.VMEM((V_HALF, HEAD, LANES), F32)],
        compiler_params=_cparams("arbitrary", "arbitrary"),
    )(kk, w, akk, k, r, v, s0)


def _n_groups(bk):
    return (bk * N_HEADS) // (LANES // 2)


def _to_chain_k(x2d, bk, tk):
    g = _n_groups(bk)
    x = x2d.reshape(bk, tk, N_HEADS, HEAD).transpose(1, 3, 0, 2).reshape(tk, HEAD, g, 1, LANES // 2)
    return jnp.broadcast_to(x, (tk, HEAD, g, 2, LANES // 2)).reshape(tk, HEAD, g * LANES)


def _to_chain_v(x2d, bk, tk):
    g = _n_groups(bk)
    x = x2d.reshape(bk, tk, N_HEADS, 2, V_HALF).transpose(1, 4, 3, 0, 2)
    x = x.reshape(tk, V_HALF, 2, g, LANES // 2).transpose(0, 1, 3, 2, 4)
    return x.reshape(tk, V_HALF, g * LANES)


def _from_chain_v(y, bk, tk):
    g = _n_groups(bk)
    y = y.reshape(tk, V_HALF, g, 2, LANES // 2).transpose(0, 1, 3, 2, 4)
    y = y.reshape(tk, V_HALF, 2, bk, N_HEADS).transpose(3, 0, 4, 2, 1)
    return y.reshape(bk * tk, D_B)


def _state_to_chain(s, bk):
    g = _n_groups(bk)
    s = s.reshape(g, LANES // 2, 2, V_HALF, HEAD).transpose(3, 4, 0, 2, 1)
    return s.reshape(V_HALF, HEAD, g * LANES)


def _state_from_chain(sc, bk):
    g = _n_groups(bk)
    s = sc.reshape(V_HALF, HEAD, g, 2, LANES // 2).transpose(2, 4, 3, 0, 1)
    return s.reshape(bk, N_HEADS, HEAD, HEAD)


def _outproj_kernel(y_ref, bv_ref, g_ref, ya_ref, x_ref, g1_ref, sh2_ref, sc2_ref, vec_ref, lnd_ref,
                    wout_ref, wrh_ref, wrl_ref, br_ref, bd_ref, x1_ref, h2_ref, gd_ref):
    vec = vec_ref[...]
    bd = bd_ref[...]
    y = y_ref[...]
    mean = _split_dot(y, bd) * (1.0 / HEAD)
    yc = y - mean
    var = _split_dot(yc * yc, bd) * (1.0 / HEAD)
    yn = yc * lax.rsqrt(var + GN_EPS) * vec[_V_LNX_G:_V_LNX_G + 1] + vec[_V_LNX_B:_V_LNX_B + 1]
    yb = (yn + bv_ref[...]) * g_ref[...]
    mix = _dot(ya_ref[...], wout_ref[:D_A, :]) + _dot(yb.astype(BF16), wout_ref[D_A:, :])
    lnd = lnd_ref[...]
    x1 = _ln_rows(ALPHA * x_ref[0] + g1_ref[0] * mix, lnd[0:1], lnd[1:2], LN_EPS)
    x1_ref[0] = x1
    h2 = x1 * (1.0 + sc2_ref[0]) + sh2_ref[0]
    h2_ref[0] = h2.astype(h2_ref.dtype)

    hi = h2.astype(BF16)
    lo = (h2 - hi.astype(F32)).astype(BF16)
    logits = _dot(hi, wrh_ref[...]) + _dot(hi, wrl_ref[...]) + _dot(lo, wrh_ref[...]) + br_ref[...]
    lane = lax.broadcasted_iota(jnp.int32, logits.shape, 1).astype(F32)
    work = logits
    tops, picks = [], []
    for _ in range(TOP_K):
        m = jnp.max(work, axis=1, keepdims=True)
        idx = jnp.min(jnp.where(work == m, lane, float(N_EXPERTS)), axis=1, keepdims=True)
        pick = lane == idx
        tops.append(m)
        picks.append(pick)
        work = jnp.where(pick, -jnp.inf, work)
    exps = [jnp.exp(m - tops[0]) for m in tops]
    den = exps[0] + exps[1] + exps[2] + exps[3]
    gd = jnp.full(logits.shape, -1.0, F32)
    for pick, e in zip(picks, exps):
        gd = jnp.where(pick, e / den, gd)
    gd_ref[0] = gd


def _outproj(y, bv, g, ya, x3, mods3, lw):
    bk, tk, _ = x3.shape
    rm = mods3.shape[1]
    tt = min(tk, ROW_TILE)
    n = bk * tk
    rmb = 1 if rm == 1 else tt
    mod_spec = lambda j: pl.BlockSpec((1, rmb, D_MODEL),
                                      (lambda b, t: (b, 0, j)) if rm == 1 else (lambda b, t: (b, t, j)))
    full = lambda shape: pl.BlockSpec(shape, lambda b, t: (0,) * len(shape))
    row = lambda width: pl.BlockSpec((tt, width), lambda b, t: (b * (tk // tt) + t, 0))
    tok = lambda width: pl.BlockSpec((1, tt, width), lambda b, t: (b, t, 0))
    return pl.pallas_call(
        _outproj_kernel,
        grid=(bk, tk // tt),
        in_specs=[row(D_B), row(D_B), row(D_B), row(D_A), tok(D_MODEL),
                  mod_spec(2), mod_spec(3), mod_spec(4),
                  full((_N_VEC, D_B)), full((2, D_MODEL)), full((D_MODEL, D_MODEL)),
                  full((D_MODEL, N_EXPERTS)), full((D_MODEL, N_EXPERTS)), full((1, N_EXPERTS)),
                  full((D_B, D_B))],
        out_specs=[tok(D_MODEL), tok(D_MODEL), tok(N_EXPERTS)],
        out_shape=[jax.ShapeDtypeStruct((bk, tk, D_MODEL), F32),
                   jax.ShapeDtypeStruct((bk, tk, D_MODEL), BF16),
                   jax.ShapeDtypeStruct((bk, tk, N_EXPERTS), F32)],
        compiler_params=_cparams("arbitrary", "arbitrary"),
    )(y, bv, g, ya, x3, mods3, mods3, mods3, lw["vec"], lw["ln1"], lw["w_out"],
      lw["wr_hi"], lw["wr_lo"], lw["b_router"], lw["bd"])


def _moe_kernel(h_ref, gdt_ref, u_ref, wup_ref, bup_ref, wdn_ref, bdn_ref, o_ref,
                rank_scr, p_buf, y_buf, fill_ref):
    e = pl.program_id(1)
    tm = h_ref.shape[0]

    def flush():
        o_ref[...] += lax.dot_general(p_buf[...], y_buf[...], (((0,), (0,)), ((), ())),
                                      preferred_element_type=F32)

    @pl.when(e == 0)
    def _():
        o_ref[...] = jnp.zeros_like(o_ref)
        p_buf[...] = jnp.zeros_like(p_buf)
        y_buf[...] = jnp.zeros_like(y_buf)
        fill_ref[0] = 0
        sel =jnp.where(gdt_ref[...] >= 0.0, 1.0, 0.0).astype(BF16)
        rank_scr[...] = _dot(sel, u_ref[...])

    gd_row = gdt_ref[pl.ds(e, 1), :]
    sel_row = gd_row >= 0.0
    count = jnp.sum(jnp.where(sel_row, 1, 0).astype(jnp.int32))
    n_blocks = (count + MOE_ROWS - 1) // MOE_ROWS
    gate_row = jnp.where(sel_row, gd_row, 0.0)
    slot_row = jnp.where(sel_row, rank_scr[pl.ds(e, 1), :].astype(jnp.int32), -1)
    row_id = lax.broadcasted_iota(jnp.int32, (MOE_ROWS, tm), 0)

    def block(b, carry):
        onehot = jnp.where(slot_row == row_id + b * MOE_ROWS, 1.0, 0.0)
        p = onehot.astype(BF16)
        xg = _dot(p, h_ref[...])
        hcat = _dot(xg.astype(BF16), wup_ref[0]) + bup_ref[0]
        glu = jnp.minimum(hcat[:, :D_FF], SWIGLU_LIMIT)
        lin = jnp.clip(hcat[:, D_FF:], -SWIGLU_LIMIT, SWIGLU_LIMIT)
        act = glu * jax.nn.sigmoid(SWIGLU_ALPHA * glu) * (lin + 1.0)
        yb = _dot(act.astype(BF16), wdn_ref[0]) + bdn_ref[0]
        gate = jnp.sum(onehot * gate_row, axis=1, keepdims=True)
        fill = fill_ref[0]
        off = pl.multiple_of(fill * MOE_ROWS, 16)
        p_buf[pl.ds(off, MOE_ROWS), :] = p
        y_buf[pl.ds(off, MOE_ROWS), :] = (yb * gate).astype(BF16)
        fill_ref[0] = fill + 1

        @pl.when(fill + 1 == MOE_FLUSH)
        def _():
            flush()
            fill_ref[0] = 0
        return carry

    lax.fori_loop(0, n_blocks, block, 0)

    @pl.when(e == pl.num_programs(1) - 1)
    def _():
        fill = fill_ref[0]
        for s in range(MOE_FLUSH):
            @pl.when(s >= fill)
            def _():
                p_buf[s * MOE_ROWS:(s + 1) * MOE_ROWS, :] = jnp.zeros((MOE_ROWS, tm), BF16)
        flush()


def _moe(h2, gdt, lw):
    n = h2.shape[0]
    tm = min(n, MOE_TILE)
    return pl.pallas_call(
        _moe_kernel,
        grid=(n // tm, N_EXPERTS),
        in_specs=[pl.BlockSpec((tm, D_MODEL), lambda i, e: (i, 0)),
                  pl.BlockSpec((N_EXPERTS, tm), lambda i, e: (0, i)),
                  pl.BlockSpec((tm, tm), lambda i, e: (0, 0)),
                  pl.BlockSpec((1, D_MODEL, 2 * D_FF), lambda i, e: (e, 0, 0)),
                  pl.BlockSpec((1, 1, 2 * D_FF), lambda i, e: (e, 0, 0)),
                  pl.BlockSpec((1, D_FF, D_MODEL), lambda i, e: (e, 0, 0)),
                  pl.BlockSpec((1, 1, D_MODEL), lambda i, e: (e, 0, 0))],
        out_specs=pl.BlockSpec((tm, D_MODEL), lambda i, e: (i, 0)),
        out_shape=jax.ShapeDtypeStruct((n, D_MODEL), F32),
        scratch_shapes=[pltpu.VMEM((N_EXPERTS, tm), F32),
                        pltpu.VMEM((MOE_FLUSH * MOE_ROWS, tm), BF16),
                        pltpu.VMEM((MOE_FLUSH * MOE_ROWS, D_MODEL), BF16),
                        pltpu.SMEM((1,), jnp.int32)],
        compiler_params=_cparams("arbitrary", "arbitrary"),
    )(h2, gdt, lw["tri"][:tm, :tm], lw["w_up"], lw["b_up"], lw["w_down"], lw["b_down"])


def _ln2_kernel(x_ref, f_ref, g2_ref, lnd_ref, o_ref):
    lnd = lnd_ref[...]
    o_ref[0] = _ln_rows(ALPHA * x_ref[0] + g2_ref[0] * f_ref[0], lnd[0:1], lnd[1:2], LN_EPS)


def _ln2(x3, f3, mods3, lw):
    bk, tk, _ = x3.shape
    rm = mods3.shape[1]
    tt = min(tk, 512)
    rmb = 1 if rm == 1 else tt
    tok = pl.BlockSpec((1, tt, D_MODEL), lambda b, t: (b, t, 0))
    return pl.pallas_call(
        _ln2_kernel,
        grid=(bk, tk // tt),
        in_specs=[tok, tok,
                  pl.BlockSpec((1, rmb, D_MODEL),
                               (lambda b, t: (b, 0, 5)) if rm == 1 else (lambda b, t: (b, t, 5))),
                  pl.BlockSpec((2, D_MODEL), lambda b, t: (0, 0))],
        out_specs=tok,
        out_shape=jax.ShapeDtypeStruct((bk, tk, D_MODEL), F32),
        compiler_params=_cparams("arbitrary", "arbitrary"),
    )(x3, f3, mods3, lw["ln2"])


def _mixer(x3, mods3, lw, seq_mode, shift3, wkv):
    bk, tk, _ = x3.shape
    if seq_mode:
        ya, g, bv, r, w, k2, v, kk, akk, shift_new = _inproj(x3, mods3, lw, True)
        nb, nt, va = bk, tk, None
        s0 = jnp.zeros((V_HALF, HEAD, _n_groups(nb) * LANES), F32)
    else:
        ya, g, bv, r, w, k2, v, kk, akk, shift_new, va = _inproj(x3, mods3, lw, False, shift3)
        nb, nt = tk, 1
        s0 = _state_to_chain(wkv, nb)
    ck = lambda z: _to_chain_k(z, nb, nt)
    y, s_t = _recurrence(ck(kk), ck(w), ck(akk), ck(k2), ck(r), _to_chain_v(v, nb, nt), s0)
    y = _from_chain_v(y, nb, nt)
    x1, h2, gd = _outproj(y, bv, g, ya, x3, mods3, lw)
    return x1, h2, gd, shift_new, _state_from_chain(s_t, nb), va


def kernel(x_prompt, x_sample, state_wkv, state_shift, c_prompt, c_sample, ln_in_g, ln_in_b, w_ada, b_ada, w_in, ln_v_g, ln_v_b, w_s, b_s, mu_shift, w0, w_decay_up, a0, w_iclr_up, w_gate_up, k_k, k_a, r_k, ln_x_g, ln_x_b, w_out, ln1_g, ln1_b, w_router, b_router, w_up, b_up, w_down, b_down, ln2_g, ln2_b):
    bp, tp, _ = x_prompt.shape
    bs = x_sample.shape[0]
    n_p, n_s = bp * tp, bs
    n_all = -(-(n_p + n_s) // MOE_TILE) * MOE_TILE

    lane_head = jnp.arange(D_B) // HEAD
    bd = (lane_head[:, None] == lane_head[None, :]).astype(BF16)
    tri = (jnp.arange(MOE_TILE)[:, None] < jnp.arange(MOE_TILE)[None, :]).astype(BF16)
    zeros_lora = jnp.zeros((LORA_W, D_B), F32)
    layers = []
    for l in range(DEPTH):
        rows = [ln_v_g[l], ln_v_b[l], w0[l], a0[l], k_k[l], k_a[l], r_k[l].reshape(D_B),
                jnp.repeat(w_s[l, :, 0, 0], HEAD), jnp.repeat(b_s[l, :, 0], HEAD), ln_x_g[l], ln_x_b[l]]
        vec = jnp.stack(rows + [jnp.zeros((D_B,), F32)] * (_N_VEC - len(rows)))
        wr_hi = w_router[l].astype(BF16)
        layers.append(dict(
            w_in=w_in[l].astype(BF16), vec=vec, mu=mu_shift[l].reshape(1, D_B_PROJ),
            wwa=jnp.concatenate([jnp.concatenate([w_decay_up[l], zeros_lora], axis=1),
                                 jnp.concatenate([zeros_lora, w_iclr_up[l]], axis=1)], axis=0).astype(BF16),
            wg=w_gate_up[l].astype(BF16), bd=bd, w_s=w_s[l],
            bsx=jnp.repeat(b_s[l].T, HEAD, axis=1),
            w_out=w_out[l].astype(BF16),
            ln1=jnp.stack([ln1_g[l], ln1_b[l]]), ln2=jnp.stack([ln2_g[l], ln2_b[l]]),
            wr_hi=wr_hi, wr_lo=(w_router[l] - wr_hi.astype(F32)).astype(BF16),
            b_router=b_router[l].reshape(1, N_EXPERTS), tri=tri,
            w_up=w_up[l].astype(BF16), b_up=b_up[l].reshape(N_EXPERTS, 1, 2 * D_FF),
            w_down=w_down[l].astype(BF16), b_down=b_down[l].reshape(N_EXPERTS, 1, D_MODEL)))

    mods = _mods(jnp.concatenate([c_prompt, c_sample], axis=0), w_ada, b_ada)
    xp = _layer_norm(x_prompt.reshape(n_p, D_MODEL), ln_in_g, ln_in_b).reshape(bp, tp, D_MODEL)
    xs = _layer_norm(x_sample.reshape(n_s, D_MODEL), ln_in_g, ln_in_b).reshape(1, bs, D_MODEL)

    wkv_p, shift_p, wkv_s, shift_s, v_s = [], [], [], [], []
    for l in range(DEPTH):
        lw = layers[l]
        mods_p = mods[l, :bp].reshape(bp, 1, 6 * D_MODEL)
        mods_s = mods[l, bp:].reshape(1, bs, 6 * D_MODEL)
        x1p, h2p, gdp, sh_p, s_p, _ = _mixer(xp, mods_p, lw, True, None, None)
        x1s, h2s, gds, sh_s, s_s, va_s = _mixer(xs, mods_s, lw, False,
                                                state_shift[l].reshape(1, bs, D_B_PROJ), state_wkv[l])
        pad = n_all - n_p - n_s
        h2 = jnp.concatenate([h2p.reshape(n_p, D_MODEL), h2s.reshape(n_s, D_MODEL),
                              jnp.zeros((pad, D_MODEL), BF16)], axis=0)
        gd = jnp.concatenate([gdp.reshape(n_p, N_EXPERTS), gds.reshape(n_s, N_EXPERTS),
                              jnp.full((pad, N_EXPERTS), -1.0, F32)], axis=0)
        ffn = _moe(h2, gd.T, lw)
        xp = _ln2(x1p, ffn[:n_p].reshape(bp, tp, D_MODEL), mods_p, lw)
        xs = _ln2(x1s, ffn[n_p:n_p + n_s].reshape(1, bs, D_MODEL), mods_s, lw)
        wkv_p.append(s_p)
        shift_p.append(sh_p.reshape(bp, D_B_PROJ))
        wkv_s.append(s_s)
        shift_s.append(sh_s.reshape(bs, D_B_PROJ))
        v_s.append(va_s.reshape(bs, 1, D_A))

    return (xp, xs.reshape(bs, 1, D_MODEL), jnp.stack(wkv_p), jnp.stack(shift_p),
            jnp.stack(wkv_s), jnp.stack(shift_s), jnp.stack(v_s))
```

```python
import functools
import math

import jax
import jax.numpy as jnp
from jax import lax
from jax.experimental import pallas as pl
from jax.experimental.pallas import tpu as pltpu

F32 = jnp.float32
BF16 = jnp.bfloat16

D_MODEL = 1024
DEPTH = 4
D_A = 512
D_B = 512
HEAD = 64
N_HEADS = 8
CHUNK = 128
LORA_W = 64
LORA_A = 64
LORA_G = 128
D_B_PROJ = 3 * D_B + LORA_W + LORA_A + LORA_G
D_PROJ = 2 * D_A + D_B_PROJ
DECAY_SCALE = math.exp(-0.5)
GN_EPS = 64e-5
LN_EPS = 1e-5
N_EXPERTS = 32
TOP_K = 4
D_FF = D_MODEL
SWIGLU_ALPHA = 1.702
SWIGLU_LIMIT = 7.0
ALPHA = (2 * DEPTH) ** 0.25

LANES = 128
SUBLANES = 8
VMEM_LIMIT = 56 * 1024 * 1024

ROW_TILE = 256
MOE_TILE = 1024
MOE_GROUP = 2
MOE_ROWS = 144
MOE_FLUSH = 8
REC_STEPS = 64

SEQ_GROUP = SUBLANES
V_QUART = HEAD // 4
HEAD_PAIRS = N_HEADS // 2


def _cparams(*sem, flags=None):
    return pltpu.CompilerParams(dimension_semantics=sem, vmem_limit_bytes=VMEM_LIMIT, flags=flags)


def _dot(a, b):
    return jnp.dot(a, b, preferred_element_type=F32)


def _ln_rows(x, g, b, eps):
    mu = jnp.mean(x, axis=-1, keepdims=True)
    xc = x - mu
    var = jnp.mean(xc * xc, axis=-1, keepdims=True)
    return xc * lax.rsqrt(var + eps) * g + b


def _split_dot(x, w_bf16):
    hi = x.astype(BF16)
    lo = (x - hi.astype(F32)).astype(BF16)
    return _dot(hi, w_bf16) + _dot(lo, w_bf16)


def _gelu_tanh(x):
    return 0.5 * x * (1.0 + jnp.tanh(0.7978845608028654 * (x + 0.044715 * (x * x * x))))


def _per_seq(x, m):
    p = m.shape[0]
    if p == x.shape[0]:
        return m
    return jnp.broadcast_to(m[None], (x.shape[0] // p, p, m.shape[1])).reshape(x.shape[0], m.shape[1])


def _ln_kernel(x_ref, g_ref, b_ref, o_ref):
    o_ref[...] = _ln_rows(x_ref[...], g_ref[...], b_ref[...], LN_EPS)


def _layer_norm(x2d, g, b):
    n = x2d.shape[0]
    tm = min(n, 512)
    return pl.pallas_call(
        _ln_kernel,
        grid=(n // tm,),
        in_specs=[pl.BlockSpec((tm, D_MODEL), lambda i: (i, 0)),
                  pl.BlockSpec((1, D_MODEL), lambda i: (0, 0)),
                  pl.BlockSpec((1, D_MODEL), lambda i: (0, 0))],
        out_specs=pl.BlockSpec((tm, D_MODEL), lambda i: (i, 0)),
        out_shape=jax.ShapeDtypeStruct((n, D_MODEL), F32),
        compiler_params=_cparams("arbitrary"),
    )(x2d, g.reshape(1, D_MODEL), b.reshape(1, D_MODEL))


def _mods_kernel(c_ref, w_ref, b_ref, o_ref):
    c = c_ref[...]
    ca = c * jax.nn.sigmoid(c)
    o_ref[0] = _dot(ca.astype(BF16), w_ref[0].astype(BF16)) + b_ref[0]


def _mods(c_all, w_ada, b_ada):
    n = c_all.shape[0]
    tn = 1536
    return pl.pallas_call(
        _mods_kernel,
        grid=(DEPTH, 6 * D_MODEL // tn),
        in_specs=[pl.BlockSpec((n, D_MODEL), lambda l, j: (0, 0)),
                  pl.BlockSpec((1, D_MODEL, tn), lambda l, j: (l, 0, j)),
                  pl.BlockSpec((1, 1, tn), lambda l, j: (l, 0, j))],
        out_specs=pl.BlockSpec((1, n, tn), lambda l, j: (l, 0, j)),
        out_shape=jax.ShapeDtypeStruct((DEPTH, n, 6 * D_MODEL), F32),
        compiler_params=_cparams("arbitrary", "arbitrary"),
    )(c_all, w_ada, b_ada.reshape(DEPTH, 1, 6 * D_MODEL))


_V_LNV_G, _V_LNV_B, _V_W0, _V_A0, _V_KK, _V_KA, _V_RK, _V_W00, _V_B0, _V_LNX_G, _V_LNX_B = range(11)
_N_VEC = 16


def _project(x, sh, sc, win_ref):
    h = x * (1.0 + _per_seq(x, sc)) + _per_seq(x, sh)
    return _dot(h.astype(BF16), win_ref[...])


def _gelu_split(proj, vec):
    ga = _gelu_tanh(proj[:, :2 * D_A])
    va = _ln_rows(ga[:, D_A:], vec[_V_LNV_G:_V_LNV_G + 1], vec[_V_LNV_B:_V_LNV_B + 1], LN_EPS)
    return ga[:, :D_A], va


def _rwkv_prep(pb, prev, vec, mu_ref, wwa_ref, wg_ref, bd_ref, outs, rows):
    g_ref, bv_ref, r_ref, w_ref, k_ref, v_ref, kk_ref, akk_ref = outs
    row = lambda i: vec[i:i + 1, :]
    xs = pb + mu_ref[...] * (prev - pb)
    r = xs[:, :D_B]
    k = xs[:, D_B:2 * D_B]
    v = xs[:, 2 * D_B:3 * D_B]
    xwa = xs[:, 3 * D_B:3 * D_B + LORA_W + LORA_A]
    xg = xs[:, 3 * D_B + LORA_W + LORA_A:]
    lane = lax.broadcasted_iota(jnp.int32, (1, LORA_W + LORA_A), 1)
    lora_in = jnp.where(lane < LORA_W, jnp.tanh(xwa), xwa)
    wa = _dot(lora_in.astype(BF16), wwa_ref[...])
    w = jnp.exp(-DECAY_SCALE * jax.nn.sigmoid(row(_V_W0) + wa[:, :D_B]))
    a = jax.nn.sigmoid(row(_V_A0) + wa[:, D_B:])
    g = _dot(jax.nn.sigmoid(xg).astype(BF16), wg_ref[...])
    bd = bd_ref[...]
    kk = k * row(_V_KK)
    kk = kk * lax.rsqrt(jnp.maximum(_split_dot(kk * kk, bd), 1e-24))
    k2 = k * (1.0 + (a - 1.0) * row(_V_KA))
    bonus = _split_dot(r * k2 * row(_V_RK), bd)
    g_ref[rows, :] = g
    bv_ref[rows, :] = bonus * v
    r_ref[rows, :] = r
    w_ref[rows, :] = w
    k_ref[rows, :] = k2
    v_ref[rows, :] = v
    kk_ref[rows, :] = kk
    akk_ref[rows, :] = kk * a


def _inproj_seq_kernel(x_ref, sh_ref, sc_ref, win_ref, vec_ref, mu_ref, wwa_ref, wg_ref, bd_ref,
                       ws_ref, bsx_ref,
                       ya_ref, g_ref, bv_ref, r_ref, w_ref, k_ref, v_ref, kk_ref, akk_ref, shift_ref,
                       carry_ref, ua_scr, va_scr, ya_scr):
    n_seq = sh_ref.shape[0]
    sub_rows = x_ref.shape[0]
    sb = pl.program_id(1)
    vec = vec_ref[...]
    outs = (g_ref, bv_ref, r_ref, w_ref, k_ref, v_ref, kk_ref, akk_ref)

    @pl.when((pl.program_id(0) == 0) & (sb == 0))
    def _():
        carry_ref[...] = jnp.zeros_like(carry_ref)

    rows = pl.ds(pl.multiple_of(sb * sub_rows, sub_rows), sub_rows)
    proj = _project(x_ref[...], sh_ref[...], sc_ref[...], win_ref)
    ua, va = _gelu_split(proj, vec)
    for j in range(D_A // LANES):
        ua_scr[j, rows, :] = ua[:, LANES * j:LANES * (j + 1)]
        va_scr[j, rows, :] = va[:, LANES * j:LANES * (j + 1)]
    pb = proj[:, 2 * D_A:]
    prev = jnp.concatenate([carry_ref[...], pb[:sub_rows - n_seq, :]], axis=0)
    carry_ref[...] = pb[sub_rows - n_seq:, :]
    shift_ref[...] = pb[sub_rows - n_seq:, :]
    _rwkv_prep(pb, prev, vec, mu_ref, wwa_ref, wg_ref, bd_ref, outs, slice(None))

    @pl.when(sb == pl.num_programs(1) - 1)
    def _():
        head_of_lane = lax.broadcasted_iota(jnp.int32, (1, D_A), 1) // HEAD
        ri = lax.broadcasted_iota(jnp.int32, (CHUNK, CHUNK), 0)
        ci = lax.broadcasted_iota(jnp.int32, (CHUNK, CHUNK), 1)
        w_causal = [jnp.where(ri >= ci, ws_ref[hd], 0.0).astype(BF16) for hd in range(N_HEADS)]
        lane_blocks = range(D_A // LANES)
        for b in range(n_seq):
            seq_rows = pl.ds(b, CHUNK, stride=n_seq)
            vb = jnp.concatenate([va_scr[j, seq_rows, :] for j in lane_blocks], axis=1)
            ub = jnp.concatenate([ua_scr[j, seq_rows, :] for j in lane_blocks], axis=1)
            s = bsx_ref[...]
            for hd in range(N_HEADS):
                v_h = jnp.where(head_of_lane == hd, vb, 0.0).astype(BF16)
                s = s + _dot(w_causal[hd], v_h)
            ya = ub * s
            for j in lane_blocks:
                ya_scr[j, seq_rows, :] = ya[:, LANES * j:LANES * (j + 1)]
        ya_ref[...] = jnp.concatenate([ya_scr[j] for j in lane_blocks], axis=1).astype(ya_ref.dtype)


def _inproj_step_kernel(x_ref, sh_ref, sc_ref, win_ref, vec_ref, mu_ref, wwa_ref, wg_ref, bd_ref,
                        prev_ref,
                        ya_ref, g_ref, bv_ref, r_ref, w_ref, k_ref, v_ref, kk_ref, akk_ref, shift_ref,
                        va_ref):
    vec = vec_ref[...]
    outs = (g_ref, bv_ref, r_ref, w_ref, k_ref, v_ref, kk_ref, akk_ref)
    proj = _project(x_ref[...], sh_ref[...], sc_ref[...], win_ref)
    ua, va = _gelu_split(proj, vec)
    va_ref[...] = va
    ya_ref[...] = (ua * (va * vec[_V_W00:_V_W00 + 1] + vec[_V_B0:_V_B0 + 1])).astype(ya_ref.dtype)
    pb = proj[:, 2 * D_A:]
    shift_ref[...] = pb
    _rwkv_prep(pb, prev_ref[...], vec, mu_ref, wwa_ref, wg_ref, bd_ref, outs, slice(None))


def _inproj(x2d, mods2, lw, n_seq, prev=None):
    n = x2d.shape[0]
    seq_mode = prev is None
    chunk_rows = CHUNK * n_seq
    tile = ROW_TILE if seq_mode else n
    n_sub = chunk_rows // tile if seq_mode else 1
    full = lambda shape: pl.BlockSpec(shape, lambda i, s: (0,) * len(shape))
    mod_spec = lambda j: pl.BlockSpec((n_seq, D_MODEL), lambda i, s: (0, j))
    row_spec = lambda width: pl.BlockSpec((tile, width), lambda i, s: (i * n_sub + s, 0))
    in_specs = [row_spec(D_MODEL), mod_spec(0), mod_spec(1),
                full((D_MODEL, D_PROJ)), full((_N_VEC, D_B)), full((1, D_B_PROJ)),
                full((LORA_W + LORA_A, 2 * D_B)), full((LORA_G, D_B)), full((D_B, D_B))]
    args = [x2d, mods2, mods2, lw["w_in"], lw["vec"], lw["mu"], lw["wwa"], lw["wg"], lw["bd"]]
    row_shape = jax.ShapeDtypeStruct((n, D_B), F32)
    out_shape = [jax.ShapeDtypeStruct((n, D_A), BF16)] + [row_shape] * 8
    if seq_mode:
        in_specs += [full((N_HEADS, CHUNK, CHUNK)), full((CHUNK, D_A))]
        args += [lw["w_s"], lw["bsx"]]
        out_specs = [pl.BlockSpec((chunk_rows, D_A), lambda i, s: (i, 0))] + [row_spec(D_B)] * 8
        out_specs += [full((n_seq, D_B_PROJ))]
        out_shape += [jax.ShapeDtypeStruct((n_seq, D_B_PROJ), F32)]
        scratch = ([pltpu.VMEM((n_seq, D_B_PROJ), F32)]
                   + [pltpu.VMEM((D_A // LANES, chunk_rows, LANES), F32)] * 3)
        body = _inproj_seq_kernel
    else:
        out_specs = [row_spec(D_A)] + [row_spec(D_B)] * 8
        in_specs += [row_spec(D_B_PROJ)]
        args += [prev]
        out_specs += [row_spec(D_B_PROJ), row_spec(D_A)]
        out_shape += [jax.ShapeDtypeStruct((n, D_B_PROJ), F32), row_shape]
        scratch = []
        body = _inproj_step_kernel
    return pl.pallas_call(
        body, grid=(n // (tile * n_sub), n_sub),
        in_specs=in_specs, out_specs=out_specs, out_shape=out_shape, scratch_shapes=scratch,
        compiler_params=_cparams("arbitrary", "arbitrary"),
    )(*args)


def _rec_kernel(kk_ref, w_ref, akk_ref, k_ref, r_ref, v_ref, s0_ref, y_ref, st_ref,
                s_scr, op_scr0, op_scr1, v_scr0, v_scr1, y_scr0, y_scr1):
    tc = pl.program_id(1)
    steps = kk_ref.shape[0]

    @pl.when(tc == 0)
    def _():
        s_scr[...] = s0_ref[0]
        y_scr0[...] = jnp.zeros_like(y_scr0)
        y_scr1[...] = jnp.zeros_like(y_scr1)

    op_scr, v_scr, y_scr = (op_scr0, op_scr1), (v_scr0, v_scr1), (y_scr0, y_scr1)

    def stacked(x):
        return jnp.concatenate([x[:, LANES * j:LANES * (j + 1)] for j in range(HEAD_PAIRS)], axis=0)

    def prep(t, buf):
        for i, ref in enumerate((kk_ref, w_ref, akk_ref, k_ref, r_ref)):
            m = stacked(ref[t])
            op_scr[buf][i] = jnp.concatenate([m] * 4, axis=0).T
        mv = stacked(v_ref[t])
        v_scr[buf][...] = jnp.concatenate(
            [mv if q == 0 else pltpu.roll(mv, LANES - V_QUART * q, 1) for q in range(4)], axis=0).T

    def update(buf):
        for par in range(2):
            keys = slice(par * HEAD, (par + 1) * HEAD)
            for vi in range(V_QUART):
                row = par * HEAD + vi
                s_old = s_scr[par, vi]
                ops = op_scr[buf]
                skk = jnp.sum(s_old * ops[0, keys, :], axis=0, keepdims=True)
                s_new = s_old * ops[1, keys, :] + (
                    v_scr[buf][row:row + 1, :] * ops[3, keys, :] - skk * ops[2, keys, :])
                s_scr[par, vi] = s_new
                y_scr[buf][row:row + 1, :] = jnp.sum(s_new * ops[4, keys, :], axis=0, keepdims=True)

    def emit(t, buf):
        yt = y_scr[buf][...].T
        acc = yt[0:32]
        for q in range(1, 4):
            acc = acc + pltpu.roll(yt[32 * q:32 * (q + 1)], V_QUART * q, 1)
        y_ref[t] = jnp.concatenate([acc[SEQ_GROUP * j:SEQ_GROUP * (j + 1)] for j in range(HEAD_PAIRS)],
                                   axis=1)

    if steps % 2:
        def step(t, carry):
            prep(t, 0)
            update(0)
            emit(t, 0)
            return carry

        lax.fori_loop(0, steps, step, 0)
    else:
        prep(0, 0)

        def pair(i, carry):
            t0 = 2 * i
            emit(jnp.maximum(t0 - 1, 0), 1)
            prep(t0 + 1, 1)
            update(0)
            emit(t0, 0)
            prep(jnp.minimum(t0 + 2, steps - 1), 0)
            update(1)
            return carry

        lax.fori_loop(0, steps // 2, pair, 0)
        emit(steps - 1, 1)

    @pl.when(tc == pl.num_programs(1) - 1)
    def _():
        st_ref[0] = s_scr[...]


def _recurrence(kk, w, akk, k, r, v, s0, n_seq):
    t = kk.shape[0] // n_seq
    groups = n_seq // SEQ_GROUP
    tc = min(t, REC_STEPS)
    as3 = lambda z: z.reshape(t, n_seq, D_B)
    spec = pl.BlockSpec((tc, SEQ_GROUP, D_B), lambda g, i: (i, g, 0))
    sspec = pl.BlockSpec((1, 2, V_QUART, HEAD, LANES), lambda g, i: (g, 0, 0, 0, 0))
    y, s_t = pl.pallas_call(
        _rec_kernel,
        grid=(groups, t // tc),
        in_specs=[spec] * 6 + [sspec],
        out_specs=[spec, sspec],
        out_shape=[jax.ShapeDtypeStruct((t, n_seq, D_B), F32),
                   jax.ShapeDtypeStruct((groups, 2, V_QUART, HEAD, LANES), F32)],
        scratch_shapes=[pltpu.VMEM((2, V_QUART, HEAD, LANES), F32),
                        pltpu.VMEM((5, LANES, LANES), F32), pltpu.VMEM((5, LANES, LANES), F32),
                        pltpu.VMEM((LANES, LANES), F32), pltpu.VMEM((LANES, LANES), F32),
                        pltpu.VMEM((LANES, LANES), F32), pltpu.VMEM((LANES, LANES), F32)],
        compiler_params=_cparams("arbitrary", "arbitrary"),
    )(as3(kk), as3(w), as3(akk), as3(k), as3(r), as3(v), s0)
    return y.reshape(t * n_seq, D_B), s_t


def _state_to_chain(s, n_seq):
    g = n_seq // SEQ_GROUP
    s = s.reshape(g, SEQ_GROUP, HEAD_PAIRS, 2, 4, V_QUART, HEAD).transpose(0, 3, 5, 6, 4, 2, 1)
    return s.reshape(g, 2, V_QUART, HEAD, LANES)


def _state_from_chain(sc, n_seq):
    g = n_seq // SEQ_GROUP
    s = sc.reshape(g, 2, V_QUART, HEAD, 4, HEAD_PAIRS, SEQ_GROUP).transpose(0, 6, 5, 1, 4, 2, 3)
    return s.reshape(n_seq, N_HEADS, HEAD, HEAD)


def _outproj_kernel(y_ref, bv_ref, g_ref, ya_ref, x_ref, g1_ref, sh2_ref, sc2_ref, vec_ref, lnd_ref,
                    wout_ref, wrh_ref, wrl_ref, br_ref, bd_ref, x1_ref, h2_ref, gd_ref):
    vec = vec_ref[...]
    bd = bd_ref[...]
    y = y_ref[...]
    mean = _split_dot(y, bd) * (1.0 / HEAD)
    yc = y - mean
    var = _split_dot(yc * yc, bd) * (1.0 / HEAD)
    yn = yc * lax.rsqrt(var + GN_EPS) * vec[_V_LNX_G:_V_LNX_G + 1] + vec[_V_LNX_B:_V_LNX_B + 1]
    yb = (yn + bv_ref[...]) * g_ref[...]
    mix = _dot(ya_ref[...], wout_ref[:D_A, :]) + _dot(yb.astype(BF16), wout_ref[D_A:, :])
    lnd = lnd_ref[...]
    x = x_ref[...]
    x1 = _ln_rows(ALPHA * x + _per_seq(x, g1_ref[...]) * mix, lnd[0:1], lnd[1:2], LN_EPS)
    x1_ref[...] = x1
    h2 = x1 * (1.0 + _per_seq(x, sc2_ref[...])) + _per_seq(x, sh2_ref[...])
    h2_ref[...] = h2.astype(h2_ref.dtype)

    hi = h2.astype(BF16)
    lo = (h2 - hi.astype(F32)).astype(BF16)
    logits = _dot(hi, wrh_ref[...]) + _dot(hi, wrl_ref[...]) + _dot(lo, wrh_ref[...]) + br_ref[...]
    lane = lax.broadcasted_iota(jnp.int32, logits.shape, 1).astype(F32)
    work = logits
    tops, picks = [], []
    for _ in range(TOP_K):
        m = jnp.max(work, axis=1, keepdims=True)
        idx = jnp.min(jnp.where(work == m, lane, float(N_EXPERTS)), axis=1, keepdims=True)
        pick = lane == idx
        tops.append(m)
        picks.append(pick)
        work = jnp.where(pick, -jnp.inf, work)
    exps = [jnp.exp(m - tops[0]) for m in tops]
    den = exps[0] + exps[1] + exps[2] + exps[3]
    gd = jnp.full(logits.shape, -1.0, F32)
    for pick, e in zip(picks, exps):
        gd = jnp.where(pick, e / den, gd)
    gd_ref[...] = gd


def _outproj(y, bv, g, ya, x2d, mods2, lw, n_seq):
    n = x2d.shape[0]
    tm = min(n, ROW_TILE)
    full = lambda shape: pl.BlockSpec(shape, lambda i: (0,) * len(shape))
    mod_spec = lambda j: pl.BlockSpec((n_seq, D_MODEL), lambda i: (0, j))
    row = lambda width: pl.BlockSpec((tm, width), lambda i: (i, 0))
    return pl.pallas_call(
        _outproj_kernel,
        grid=(n // tm,),
        in_specs=[row(D_B), row(D_B), row(D_B), row(D_A), row(D_MODEL),
                  mod_spec(2), mod_spec(3), mod_spec(4),
                  full((_N_VEC, D_B)), full((2, D_MODEL)), full((D_MODEL, D_MODEL)),
                  full((D_MODEL, N_EXPERTS)), full((D_MODEL, N_EXPERTS)), full((1, N_EXPERTS)),
                  full((D_B, D_B))],
        out_specs=[row(D_MODEL), row(D_MODEL), row(N_EXPERTS)],
        out_shape=[jax.ShapeDtypeStruct((n, D_MODEL), F32),
                   jax.ShapeDtypeStruct((n, D_MODEL), BF16),
                   jax.ShapeDtypeStruct((n, N_EXPERTS), F32)],
        compiler_params=_cparams("arbitrary"),
    )(y, bv, g, ya, x2d, mods2, mods2, mods2, lw["vec"], lw["ln1"], lw["w_out"],
      lw["wr_hi"], lw["wr_lo"], lw["b_router"], lw["bd"])


def _moe_kernel(h_ref, gdt_ref, u_ref, wup_ref, bup_ref, wdn_ref, bdn_ref, o_ref,
                rank_scr, p_buf, y_buf, fill_ref):
    e = pl.program_id(1)
    tm = MOE_TILE

    for sub in range(h_ref.shape[0] // tm):
        tok = slice(sub * tm, (sub + 1) * tm)

        def flush(sub=sub, tok=tok):
            o_ref[tok, :] += lax.dot_general(p_buf[sub], y_buf[sub], (((0,), (0,)), ((), ())),
                                             preferred_element_type=F32)

        @pl.when(e == 0)
        def _(sub=sub, tok=tok):
            o_ref[tok, :] = jnp.zeros((tm, D_MODEL), F32)
            p_buf[sub] = jnp.zeros(p_buf.shape[1:], BF16)
            y_buf[sub] = jnp.zeros(y_buf.shape[1:], BF16)
            fill_ref[sub] = 0
            sel = jnp.where(gdt_ref[:, tok] >= 0.0, 1.0, 0.0).astype(BF16)
            rank_scr[sub] = _dot(sel, u_ref[...])

        gd_row = gdt_ref[pl.ds(e, 1), tok]
        sel_row = gd_row >= 0.0
        count = jnp.sum(jnp.where(sel_row, 1, 0).astype(jnp.int32))
        n_blocks = (count + MOE_ROWS - 1) // MOE_ROWS
        gate_row = jnp.where(sel_row, gd_row, 0.0)
        slot_row = jnp.where(sel_row, rank_scr[sub, pl.ds(e, 1), :].astype(jnp.int32), -1)
        row_id = lax.broadcasted_iota(jnp.int32, (MOE_ROWS, tm), 0)

        def block(b, carry, sub=sub, tok=tok, slot_row=slot_row, gate_row=gate_row, row_id=row_id,
                  flush=flush):
            onehot = jnp.where(slot_row == row_id + b * MOE_ROWS, 1.0, 0.0)
            p = onehot.astype(BF16)
            xg = _dot(p, h_ref[tok, :])
            hcat = _dot(xg.astype(BF16), wup_ref[0]) + bup_ref[0]
            glu = jnp.minimum(hcat[:, :D_FF], SWIGLU_LIMIT)
            lin = jnp.clip(hcat[:, D_FF:], -SWIGLU_LIMIT, SWIGLU_LIMIT)
            act = glu * jax.nn.sigmoid(SWIGLU_ALPHA * glu) * (lin + 1.0)
            yb = _dot(act.astype(BF16), wdn_ref[0]) + bdn_ref[0]
            gate = jnp.sum(onehot * gate_row, axis=1, keepdims=True)
            fill = fill_ref[sub]
            off = pl.multiple_of(fill * MOE_ROWS, 16)
            p_buf[sub, pl.ds(off, MOE_ROWS), :] = p
            y_buf[sub, pl.ds(off, MOE_ROWS), :] = (yb * gate).astype(BF16)
            fill_ref[sub] = fill + 1

            @pl.when(fill + 1 == MOE_FLUSH)
            def _():
                flush()
                fill_ref[sub] = 0
            return carry

        lax.fori_loop(0, n_blocks, block, 0)

        @pl.when(e == pl.num_programs(1) - 1)
        def _(sub=sub, flush=flush):
            fill = fill_ref[sub]
            for s in range(MOE_FLUSH):
                @pl.when(s >= fill)
                def _(s=s):
                    p_buf[sub, s * MOE_ROWS:(s + 1) * MOE_ROWS, :] = jnp.zeros((MOE_ROWS, tm), BF16)
            flush()


def _moe(h2, gdt, lw):
    n = h2.shape[0]
    rows = MOE_GROUP * MOE_TILE
    return pl.pallas_call(
        _moe_kernel,
        grid=(n // rows, N_EXPERTS),
        in_specs=[pl.BlockSpec((rows, D_MODEL), lambda i, e: (i, 0)),
                  pl.BlockSpec((N_EXPERTS, rows), lambda i, e: (0, i)),
                  pl.BlockSpec((MOE_TILE, MOE_TILE), lambda i, e: (0, 0)),
                  pl.BlockSpec((1, D_MODEL, 2 * D_FF), lambda i, e: (e, 0, 0)),
                  pl.BlockSpec((1, 1, 2 * D_FF), lambda i, e: (e, 0, 0)),
                  pl.BlockSpec((1, D_FF, D_MODEL), lambda i, e: (e, 0, 0)),
                  pl.BlockSpec((1, 1, D_MODEL), lambda i, e: (e, 0, 0))],
        out_specs=pl.BlockSpec((rows, D_MODEL), lambda i, e: (i, 0)),
        out_shape=jax.ShapeDtypeStruct((n, D_MODEL), F32),
        scratch_shapes=[pltpu.VMEM((MOE_GROUP, N_EXPERTS, MOE_TILE), F32),
                        pltpu.VMEM((MOE_GROUP, MOE_FLUSH * MOE_ROWS, MOE_TILE), BF16),
                        pltpu.VMEM((MOE_GROUP, MOE_FLUSH * MOE_ROWS, D_MODEL), BF16),
                        pltpu.SMEM((MOE_GROUP,), jnp.int32)],
        compiler_params=_cparams("arbitrary", "arbitrary"),
    )(h2, gdt, lw["tri"], lw["w_up"], lw["b_up"], lw["w_down"], lw["b_down"])


def _ln2_kernel(x_ref, f_ref, g2_ref, lnd_ref, o_ref):
    lnd = lnd_ref[...]
    x = x_ref[...]
    o_ref[...] = _ln_rows(ALPHA * x + _per_seq(x, g2_ref[...]) * f_ref[...], lnd[0:1], lnd[1:2], LN_EPS)


def _ln2(x2d, ffn_all, row0, mods2, lw, n_seq):
    n = x2d.shape[0]
    tm = min(n, 512)
    blk0 = row0 // tm
    return pl.pallas_call(
        _ln2_kernel,
        grid=(n // tm,),
        in_specs=[pl.BlockSpec((tm, D_MODEL), lambda i: (i, 0)),
                  pl.BlockSpec((tm, D_MODEL), lambda i: (blk0 + i, 0)),
                  pl.BlockSpec((n_seq, D_MODEL), lambda i: (0, 5)),
                  pl.BlockSpec((2, D_MODEL), lambda i: (0, 0))],
        out_specs=pl.BlockSpec((tm, D_MODEL), lambda i: (i, 0)),
        out_shape=jax.ShapeDtypeStruct((n, D_MODEL), F32),
        compiler_params=_cparams("arbitrary"),
    )(x2d, ffn_all, mods2, lw["ln2"])


def _mixer(x2d, mods2, lw, n_seq, shift, wkv):
    if shift is None:
        ya, g, bv, r, w, k2, v, kk, akk, shift_new = _inproj(x2d, mods2, lw, n_seq)
        s0 = jnp.zeros((n_seq // SEQ_GROUP, 2, V_QUART, HEAD, LANES), F32)
        va = None
    else:
        ya, g, bv, r, w, k2, v, kk, akk, shift_new, va = _inproj(x2d, mods2, lw, n_seq, shift)
        s0 = _state_to_chain(wkv, n_seq)
    y, s_t = _recurrence(kk, w, akk, k2, r, v, s0, n_seq)
    x1, h2, gd = _outproj(y, bv, g, ya, x2d, mods2, lw, n_seq)
    return x1, h2, gd, shift_new, _state_from_chain(s_t, n_seq), va


def kernel(x_prompt, x_sample, state_wkv, state_shift, c_prompt, c_sample, ln_in_g, ln_in_b, w_ada, b_ada, w_in, ln_v_g, ln_v_b, w_s, b_s, mu_shift, w0, w_decay_up, a0, w_iclr_up, w_gate_up, k_k, k_a, r_k, ln_x_g, ln_x_b, w_out, ln1_g, ln1_b, w_router, b_router, w_up, b_up, w_down, b_down, ln2_g, ln2_b):
    bp, tp, _ = x_prompt.shape
    bs = x_sample.shape[0]
    n_p, n_s = bp * tp, bs
    moe_rows = MOE_GROUP * MOE_TILE
    n_all = -(-(n_p + n_s) // moe_rows) * moe_rows

    lane_head = jnp.arange(D_B) // HEAD
    bd = (lane_head[:, None] == lane_head[None, :]).astype(BF16)
    tri = (jnp.arange(MOE_TILE)[:, None] < jnp.arange(MOE_TILE)[None, :]).astype(BF16)
    zeros_lora = jnp.zeros((LORA_W, D_B), F32)
    layers = []
    for l in range(DEPTH):
        rows = [ln_v_g[l], ln_v_b[l], w0[l], a0[l], k_k[l], k_a[l], r_k[l].reshape(D_B),
                jnp.repeat(w_s[l, :, 0, 0], HEAD), jnp.repeat(b_s[l, :, 0], HEAD), ln_x_g[l], ln_x_b[l]]
        vec = jnp.stack(rows + [jnp.zeros((D_B,), F32)] * (_N_VEC - len(rows)))
        wr_hi = w_router[l].astype(BF16)
        layers.append(dict(
            w_in=w_in[l].astype(BF16), vec=vec, mu=mu_shift[l].reshape(1, D_B_PROJ),
            wwa=jnp.concatenate([jnp.concatenate([w_decay_up[l], zeros_lora], axis=1),
                                 jnp.concatenate([zeros_lora, w_iclr_up[l]], axis=1)], axis=0).astype(BF16),
            wg=w_gate_up[l].astype(BF16), bd=bd, w_s=w_s[l],
            bsx=jnp.repeat(b_s[l].T, HEAD, axis=1),
            w_out=w_out[l].astype(BF16),
            ln1=jnp.stack([ln1_g[l], ln1_b[l]]), ln2=jnp.stack([ln2_g[l], ln2_b[l]]),
            wr_hi=wr_hi, wr_lo=(w_router[l] - wr_hi.astype(F32)).astype(BF16),
            b_router=b_router[l].reshape(1, N_EXPERTS), tri=tri,
            w_up=w_up[l].astype(BF16), b_up=b_up[l].reshape(N_EXPERTS, 1, 2 * D_FF),
            w_down=w_down[l].astype(BF16), b_down=b_down[l].reshape(N_EXPERTS, 1, D_MODEL)))

    mods = _mods(jnp.concatenate([c_prompt, c_sample], axis=0), w_ada, b_ada)
    xp = _layer_norm(x_prompt.transpose(1, 0, 2).reshape(n_p, D_MODEL), ln_in_g, ln_in_b)
    xs = _layer_norm(x_sample.reshape(n_s, D_MODEL), ln_in_g, ln_in_b)

    pad = n_all - n_p - n_s
    wkv_p, shift_p, wkv_s, shift_s, v_s = [], [], [], [], []
    for l in range(DEPTH):
        lw = layers[l]
        mods_p, mods_s = mods[l, :bp], mods[l, bp:]
        x1p, h2p, gdp, sh_p, s_p, _ = _mixer(xp, mods_p, lw, bp, None, None)
        x1s, h2s, gds, sh_s, s_s, va_s = _mixer(xs, mods_s, lw, bs, state_shift[l], state_wkv[l])
        h2 = jnp.concatenate([h2p, h2s, jnp.zeros((pad, D_MODEL), BF16)], axis=0)
        gd = jnp.concatenate([gdp, gds, jnp.full((pad, N_EXPERTS), -1.0, F32)], axis=0)
        ffn = _moe(h2, gd.T, lw)
        xp = _ln2(x1p, ffn, 0, mods_p, lw, bp)
        xs = _ln2(x1s, ffn, n_p, mods_s, lw, bs)
        wkv_p.append(s_p)
        shift_p.append(sh_p)
        wkv_s.append(s_s)
        shift_s.append(sh_s)
        v_s.append(va_s.reshape(bs, 1, D_A))

    y_prompt = xp.reshape(tp, bp, D_MODEL).transpose(1, 0, 2)
    return (y_prompt, xs.reshape(bs, 1, D_MODEL), jnp.stack(wkv_p), jnp.stack(shift_p),
            jnp.stack(wkv_s), jnp.stack(shift_s), jnp.stack(v_s))
```

```python
import functools
import math

import jax
import jax.numpy as jnp
from jax import lax
from jax.experimental import pallas as pl
from jax.experimental.pallas import tpu as pltpu

F32 = jnp.float32
BF16 = jnp.bfloat16

D_MODEL = 1024
DEPTH = 4
D_A = 512
D_B = 512
HEAD = 64
N_HEADS = 8
CHUNK = 128
LORA_W = 64
LORA_A = 64
LORA_G = 128
D_B_PROJ = 3 * D_B + LORA_W + LORA_A + LORA_G
D_PROJ = 2 * D_A + D_B_PROJ
DECAY_SCALE = math.exp(-0.5)
GN_EPS = 64e-5
LN_EPS = 1e-5
N_EXPERTS = 32
TOP_K = 4
D_FF = D_MODEL
SWIGLU_ALPHA = 1.702
SWIGLU_LIMIT = 7.0
ALPHA = (2 * DEPTH) ** 0.25

LANES = 128
SUBLANES = 8
VMEM_LIMIT = 56 * 1024 * 1024

ROW_TILE = 256
MOE_TILE = 1024
MOE_GROUP = 2
MOE_ROWS = 144
MOE_FLUSH = 7
REC_STEPS = 64

SEQ_GROUP = SUBLANES
V_QUART = HEAD // 4
HEAD_PAIRS = N_HEADS // 2


def _cparams(*sem, flags=None):
    return pltpu.CompilerParams(dimension_semantics=sem, vmem_limit_bytes=VMEM_LIMIT, flags=flags)


def _dot(a, b):
    return jnp.dot(a, b, preferred_element_type=F32)


def _ln_rows(x, g, b, eps):
    mu = jnp.mean(x, axis=-1, keepdims=True)
    xc = x - mu
    var = jnp.mean(xc * xc, axis=-1, keepdims=True)
    return xc * lax.rsqrt(var + eps) * g + b


def _split_dot(x, w_bf16):
    hi = x.astype(BF16)
    lo = (x - hi.astype(F32)).astype(BF16)
    return _dot(hi, w_bf16) + _dot(lo, w_bf16)


def _gelu_tanh(x):
    return 0.5 * x * (1.0 + jnp.tanh(0.7978845608028654 * (x + 0.044715 * (x * x * x))))


def _per_seq(x, m):
    p = m.shape[0]
    if p == x.shape[0]:
        return m
    return jnp.broadcast_to(m[None], (x.shape[0] // p, p, m.shape[1])).reshape(x.shape[0], m.shape[1])


def _ln_kernel(x_ref, g_ref, b_ref, o_ref):
    o_ref[...] = _ln_rows(x_ref[...], g_ref[...], b_ref[...], LN_EPS)


def _layer_norm(x2d, g, b):
    n = x2d.shape[0]
    tm = min(n, 512)
    return pl.pallas_call(
        _ln_kernel,
        grid=(n // tm,),
        in_specs=[pl.BlockSpec((tm, D_MODEL), lambda i: (i, 0)),
                  pl.BlockSpec((1, D_MODEL), lambda i: (0, 0)),
                  pl.BlockSpec((1, D_MODEL), lambda i: (0, 0))],
        out_specs=pl.BlockSpec((tm, D_MODEL), lambda i: (i, 0)),
        out_shape=jax.ShapeDtypeStruct((n, D_MODEL), F32),
        compiler_params=_cparams("arbitrary"),
    )(x2d, g.reshape(1, D_MODEL), b.reshape(1, D_MODEL))


def _mods_kernel(c_ref, w_ref, b_ref, o_ref):
    c = c_ref[...]
    ca = c * jax.nn.sigmoid(c)
    o_ref[0] = _dot(ca.astype(BF16), w_ref[0].astype(BF16)) + b_ref[0]


def _mods(c_all, w_ada, b_ada):
    n = c_all.shape[0]
    tn = 1536
    return pl.pallas_call(
        _mods_kernel,
        grid=(DEPTH, 6 * D_MODEL // tn),
        in_specs=[pl.BlockSpec((n, D_MODEL), lambda l, j: (0, 0)),
                  pl.BlockSpec((1, D_MODEL, tn), lambda l, j: (l, 0, j)),
                  pl.BlockSpec((1, 1, tn), lambda l, j: (l, 0, j))],
        out_specs=pl.BlockSpec((1, n, tn), lambda l, j: (l, 0, j)),
        out_shape=jax.ShapeDtypeStruct((DEPTH, n, 6 * D_MODEL), F32),
        compiler_params=_cparams("arbitrary", "arbitrary"),
    )(c_all, w_ada, b_ada.reshape(DEPTH, 1, 6 * D_MODEL))


_V_LNV_G, _V_LNV_B, _V_W0, _V_A0, _V_KK, _V_KA, _V_RK, _V_W00, _V_B0, _V_LNX_G, _V_LNX_B = range(11)
_N_VEC = 16


def _project(x, sh, sc, win_ref):
    h = x * (1.0 + _per_seq(x, sc)) + _per_seq(x, sh)
    return _dot(h.astype(BF16), win_ref[...])


def _gelu_split(proj, vec):
    ga = _gelu_tanh(proj[:, :2 * D_A])
    va = _ln_rows(ga[:, D_A:], vec[_V_LNV_G:_V_LNV_G + 1], vec[_V_LNV_B:_V_LNV_B + 1], LN_EPS)
    return ga[:, :D_A], va


def _rwkv_prep(pb, prev, vec, mu_ref, wwa_ref, wg_ref, bd_ref, outs, rows):
    g_ref, bv_ref, r_ref, w_ref, k_ref, v_ref, kk_ref, akk_ref = outs
    row = lambda i: vec[i:i + 1, :]
    xs = pb + mu_ref[...] * (prev - pb)
    r = xs[:, :D_B]
    k = xs[:, D_B:2 * D_B]
    v = xs[:, 2 * D_B:3 * D_B]
    xwa = xs[:, 3 * D_B:3 * D_B + LORA_W + LORA_A]
    xg = xs[:, 3 * D_B + LORA_W + LORA_A:]
    lane = lax.broadcasted_iota(jnp.int32, (1, LORA_W + LORA_A), 1)
    lora_in = jnp.where(lane < LORA_W, jnp.tanh(xwa), xwa)
    wa = _dot(lora_in.astype(BF16), wwa_ref[...])
    w = jnp.exp(-DECAY_SCALE * jax.nn.sigmoid(row(_V_W0) + wa[:, :D_B]))
    a = jax.nn.sigmoid(row(_V_A0) + wa[:, D_B:])
    g = _dot(jax.nn.sigmoid(xg).astype(BF16), wg_ref[...])
    bd = bd_ref[...]
    kk = k * row(_V_KK)
    kk = kk * lax.rsqrt(jnp.maximum(_split_dot(kk * kk, bd), 1e-24))
    k2 = k * (1.0 + (a - 1.0) * row(_V_KA))
    bonus = _split_dot(r * k2 * row(_V_RK), bd)
    g_ref[rows, :] = g
    bv_ref[rows, :] = bonus * v
    r_ref[rows, :] = r
    w_ref[rows, :] = w
    k_ref[rows, :] = k2
    v_ref[rows, :] = v
    kk_ref[rows, :] = kk
    akk_ref[rows, :] = kk * a


def _inproj_seq_kernel(x_ref, sh_ref, sc_ref, win_ref, vec_ref, mu_ref, wwa_ref, wg_ref, bd_ref,
                       ws_ref, bsx_ref,
                       ya_ref, g_ref, bv_ref, r_ref, w_ref, k_ref, v_ref, kk_ref, akk_ref, shift_ref,
                       carry_ref, ua_scr, va_scr, ya_scr):
    n_seq = sh_ref.shape[0]
    sub_rows = x_ref.shape[0]
    sb = pl.program_id(1)
    vec = vec_ref[...]
    outs = (g_ref, bv_ref, r_ref, w_ref, k_ref, v_ref, kk_ref, akk_ref)

    @pl.when((pl.program_id(0) == 0) & (sb == 0))
    def _():
        carry_ref[...] = jnp.zeros_like(carry_ref)

    rows = pl.ds(pl.multiple_of(sb * sub_rows, sub_rows), sub_rows)
    proj = _project(x_ref[...], sh_ref[...], sc_ref[...], win_ref)
    ua, va = _gelu_split(proj, vec)
    for j in range(D_A // LANES):
        ua_scr[j, rows, :] = ua[:, LANES * j:LANES * (j + 1)]
        va_scr[j, rows, :] = va[:, LANES * j:LANES * (j + 1)]
    pb = proj[:, 2 * D_A:]
    prev = jnp.concatenate([carry_ref[...], pb[:sub_rows - n_seq, :]], axis=0)
    carry_ref[...] = pb[sub_rows - n_seq:, :]
    shift_ref[...] = pb[sub_rows - n_seq:, :]
    _rwkv_prep(pb, prev, vec, mu_ref, wwa_ref, wg_ref, bd_ref, outs, slice(None))

    @pl.when(sb == pl.num_programs(1) - 1)
    def _():
        head_of_lane = lax.broadcasted_iota(jnp.int32, (1, D_A), 1) // HEAD
        ri = lax.broadcasted_iota(jnp.int32, (CHUNK, CHUNK), 0)
        ci = lax.broadcasted_iota(jnp.int32, (CHUNK, CHUNK), 1)
        w_causal = [jnp.where(ri >= ci, ws_ref[hd], 0.0).astype(BF16) for hd in range(N_HEADS)]
        lane_blocks = range(D_A // LANES)
        for b in range(n_seq):
            seq_rows = pl.ds(b, CHUNK, stride=n_seq)
            vb = jnp.concatenate([va_scr[j, seq_rows, :] for j in lane_blocks], axis=1)
            ub = jnp.concatenate([ua_scr[j, seq_rows, :] for j in lane_blocks], axis=1)
            s = bsx_ref[...]
            for hd in range(N_HEADS):
                v_h = jnp.where(head_of_lane == hd, vb, 0.0).astype(BF16)
                s = s + _dot(w_causal[hd], v_h)
            ya = ub * s
            for j in lane_blocks:
                ya_scr[j, seq_rows, :] = ya[:, LANES * j:LANES * (j + 1)]
        ya_ref[...] = jnp.concatenate([ya_scr[j] for j in lane_blocks], axis=1).astype(ya_ref.dtype)


def _inproj_step_kernel(x_ref, sh_ref, sc_ref, win_ref, vec_ref, mu_ref, wwa_ref, wg_ref, bd_ref,
                        prev_ref,
                        ya_ref, g_ref, bv_ref, r_ref, w_ref, k_ref, v_ref, kk_ref, akk_ref, shift_ref,
                        va_ref):
    vec = vec_ref[...]
    outs = (g_ref, bv_ref, r_ref, w_ref, k_ref, v_ref, kk_ref, akk_ref)
    proj = _project(x_ref[...], sh_ref[...], sc_ref[...], win_ref)
    ua, va = _gelu_split(proj, vec)
    va_ref[...] = va
    ya_ref[...] = (ua * (va * vec[_V_W00:_V_W00 + 1] + vec[_V_B0:_V_B0 + 1])).astype(ya_ref.dtype)
    pb = proj[:, 2 * D_A:]
    shift_ref[...] = pb
    _rwkv_prep(pb, prev_ref[...], vec, mu_ref, wwa_ref, wg_ref, bd_ref, outs, slice(None))


def _inproj(x2d, mods2, lw, n_seq, prev=None):
    n = x2d.shape[0]
    seq_mode = prev is None
    chunk_rows = CHUNK * n_seq
    tile = ROW_TILE if seq_mode else n
    n_sub = chunk_rows // tile if seq_mode else 1
    full = lambda shape: pl.BlockSpec(shape, lambda i, s: (0,) * len(shape))
    mod_spec = lambda j: pl.BlockSpec((n_seq, D_MODEL), lambda i, s: (0, j))
    row_spec = lambda width: pl.BlockSpec((tile, width), lambda i, s: (i * n_sub + s, 0))
    in_specs = [row_spec(D_MODEL), mod_spec(0), mod_spec(1),
                full((D_MODEL, D_PROJ)), full((_N_VEC, D_B)), full((1, D_B_PROJ)),
                full((LORA_W + LORA_A, 2 * D_B)), full((LORA_G, D_B)), full((D_B, D_B))]
    args = [x2d, mods2, mods2, lw["w_in"], lw["vec"], lw["mu"], lw["wwa"], lw["wg"], lw["bd"]]
    row_shape = jax.ShapeDtypeStruct((n, D_B), F32)
    out_shape = [jax.ShapeDtypeStruct((n, D_A), BF16)] + [row_shape] * 8
    if seq_mode:
        in_specs += [full((N_HEADS, CHUNK, CHUNK)), full((CHUNK, D_A))]
        args += [lw["w_s"], lw["bsx"]]
        out_specs = [pl.BlockSpec((chunk_rows, D_A), lambda i, s: (i, 0))] + [row_spec(D_B)] * 8
        out_specs += [full((n_seq, D_B_PROJ))]
        out_shape += [jax.ShapeDtypeStruct((n_seq, D_B_PROJ), F32)]
        scratch = ([pltpu.VMEM((n_seq, D_B_PROJ), F32)]
                   + [pltpu.VMEM((D_A // LANES, chunk_rows, LANES), F32)] * 3)
        body = _inproj_seq_kernel
    else:
        out_specs = [row_spec(D_A)] + [row_spec(D_B)] * 8
        in_specs += [row_spec(D_B_PROJ)]
        args += [prev]
        out_specs += [row_spec(D_B_PROJ), row_spec(D_A)]
        out_shape += [jax.ShapeDtypeStruct((n, D_B_PROJ), F32), row_shape]
        scratch = []
        body = _inproj_step_kernel
    return pl.pallas_call(
        body, grid=(n // (tile * n_sub), n_sub),
        in_specs=in_specs, out_specs=out_specs, out_shape=out_shape, scratch_shapes=scratch,
        compiler_params=_cparams("arbitrary", "arbitrary"),
    )(*args)


def _rec_kernel(kk_ref, w_ref, akk_ref, k_ref, r_ref, v_ref, s0_ref, y_ref, st_ref,
                s_scr, op_scr0, op_scr1, v_scr0, v_scr1, y_scr0, y_scr1):
    tc = pl.program_id(1)
    steps = kk_ref.shape[0]

    @pl.when(tc == 0)
    def _():
        s_scr[...] = s0_ref[0]
        y_scr0[...] = jnp.zeros_like(y_scr0)
        y_scr1[...] = jnp.zeros_like(y_scr1)

    op_scr, v_scr, y_scr = (op_scr0, op_scr1), (v_scr0, v_scr1), (y_scr0, y_scr1)

    def stacked(x):
        return jnp.concatenate([x[:, LANES * j:LANES * (j + 1)] for j in range(HEAD_PAIRS)], axis=0)

    op_refs = (kk_ref, w_ref, akk_ref, k_ref, r_ref)
    lane_quarter = lax.broadcasted_iota(jnp.int32, (1, LANES), 1) // (LANES // 4)

    def prep_op(t, buf, i):
        m = stacked(op_refs[i][t])
        op_scr[buf][i] = jnp.concatenate([m] * 4, axis=0).T

    def prep_v(t, buf):
        mv = stacked(v_ref[t])
        v_scr[buf][...] = jnp.concatenate(
            [mv if q == 0 else pltpu.roll(mv, LANES - V_QUART * q, 1) for q in range(4)], axis=0).T

    def update(buf, part, n_parts):
        items = [(par, vi) for par in range(2) for vi in range(V_QUART)]
        per = len(items) // n_parts
        for par, vi in items[part * per:(part + 1) * per]:
            keys = slice(par * HEAD, (par + 1) * HEAD)
            row = par * HEAD + vi
            s_old = s_scr[par, vi]
            ops = op_scr[buf]
            skk = jnp.sum(s_old * ops[0, keys, :], axis=0, keepdims=True)
            s_new = s_old * ops[1, keys, :] + (
                v_scr[buf][row:row + 1, :] * ops[3, keys, :] - skk * ops[2, keys, :])
            s_scr[par, vi] = s_new
            y_row = jnp.sum(s_new * ops[4, keys, :], axis=0, keepdims=True)
            for q in range(4):
                y_scr[buf][row + V_QUART * q:row + V_QUART * q + 1, :] = jnp.where(
                    lane_quarter == q, y_row, 0.0)

    def out_rows(buf):
        yt = y_scr[buf][...].T
        acc = (yt[0:32] + yt[32:64]) + (yt[64:96] + yt[96:128])
        return jnp.concatenate([acc[SEQ_GROUP * j:SEQ_GROUP * (j + 1)] for j in range(HEAD_PAIRS)], axis=1)

    def emit(t, buf):
        y_ref[t] = out_rows(buf)

    if steps % 2:
        def step(t, carry):
            for i in range(len(op_refs)):
                prep_op(t, 0, i)
            prep_v(t, 0)
            update(0, 0, 1)
            emit(t, 0)
            return carry

        lax.fori_loop(0, steps, step, 0)
    else:
        for i in range(len(op_refs)):
            prep_op(0, 0, i)
        prep_v(0, 0)

        def half(t_cur, t_next, cur, nxt):
            rows_prev = out_rows(nxt)
            prep_op(t_next, nxt, 0)
            update(cur, 0, 4)
            prep_op(t_next, nxt, 1)
            prep_op(t_next, nxt, 2)
            update(cur, 1, 4)
            prep_op(t_next, nxt, 3)
            prep_op(t_next, nxt, 4)
            update(cur, 2, 4)
            prep_v(t_next, nxt)
            update(cur, 3, 4)
            y_ref[jnp.maximum(t_cur - 1, 0)] = rows_prev

        def pair(i, carry):
            t0 = 2 * i
            half(t0, t0 + 1, 0, 1)
            half(t0 + 1, jnp.minimum(t0 + 2, steps - 1), 1, 0)
            return carry

        lax.fori_loop(0, steps // 2, pair, 0)
        emit(steps - 1, 1)

    @pl.when(tc == pl.num_programs(1) - 1)
    def _():
        st_ref[0] = s_scr[...]


def _recurrence(kk, w, akk, k, r, v, s0, n_seq):
    t = kk.shape[0] // n_seq
    groups = n_seq // SEQ_GROUP
    tc = min(t, REC_STEPS)
    as3 = lambda z: z.reshape(t, n_seq, D_B)
    spec = pl.BlockSpec((tc, SEQ_GROUP, D_B), lambda g, i: (i, g, 0))
    sspec = pl.BlockSpec((1, 2, V_QUART, HEAD, LANES), lambda g, i: (g, 0, 0, 0, 0))
    y, s_t = pl.pallas_call(
        _rec_kernel,
        grid=(groups, t // tc),
        in_specs=[spec] * 6 + [sspec],
        out_specs=[spec, sspec],
        out_shape=[jax.ShapeDtypeStruct((t, n_seq, D_B), F32),
                   jax.ShapeDtypeStruct((groups, 2, V_QUART, HEAD, LANES), F32)],
        scratch_shapes=[pltpu.VMEM((2, V_QUART, HEAD, LANES), F32),
                        pltpu.VMEM((5, LANES, LANES), F32), pltpu.VMEM((5, LANES, LANES), F32),
                        pltpu.VMEM((LANES, LANES), F32), pltpu.VMEM((LANES, LANES), F32),
                        pltpu.VMEM((LANES, LANES), F32), pltpu.VMEM((LANES, LANES), F32)],
        compiler_params=_cparams("arbitrary", "arbitrary"),
    )(as3(kk), as3(w), as3(akk), as3(k), as3(r), as3(v), s0)
    return y.reshape(t * n_seq, D_B), s_t


def _state_to_chain(s, n_seq):
    g = n_seq // SEQ_GROUP
    s = s.reshape(g, SEQ_GROUP, HEAD_PAIRS, 2, 4, V_QUART, HEAD).transpose(0, 3, 5, 6, 4, 2, 1)
    return s.reshape(g, 2, V_QUART, HEAD, LANES)


def _state_from_chain(sc, n_seq):
    g = n_seq // SEQ_GROUP
    s = sc.reshape(g, 2, V_QUART, HEAD, 4, HEAD_PAIRS, SEQ_GROUP).transpose(0, 6, 5, 1, 4, 2, 3)
    return s.reshape(n_seq, N_HEADS, HEAD, HEAD)


def _outproj_kernel(y_ref, bv_ref, g_ref, ya_ref, x_ref, g1_ref, sh2_ref, sc2_ref, vec_ref, lnd_ref,
                    wout_ref, wrh_ref, wrl_ref, br_ref, bd_ref, x1_ref, h2_ref, gd_ref):
    vec = vec_ref[...]
    bd = bd_ref[...]
    y = y_ref[...]
    mean = _split_dot(y, bd) * (1.0 / HEAD)
    yc = y - mean
    var = _split_dot(yc * yc, bd) * (1.0 / HEAD)
    yn = yc * lax.rsqrt(var + GN_EPS) * vec[_V_LNX_G:_V_LNX_G + 1] + vec[_V_LNX_B:_V_LNX_B + 1]
    yb = (yn + bv_ref[...]) * g_ref[...]
    mix = _dot(ya_ref[...], wout_ref[:D_A, :]) + _dot(yb.astype(BF16), wout_ref[D_A:, :])
    lnd = lnd_ref[...]
    x = x_ref[...]
    x1 = _ln_rows(ALPHA * x + _per_seq(x, g1_ref[...]) * mix, lnd[0:1], lnd[1:2], LN_EPS)
    x1_ref[...] = x1
    h2 = x1 * (1.0 + _per_seq(x, sc2_ref[...])) + _per_seq(x, sh2_ref[...])
    h2_ref[...] = h2.astype(h2_ref.dtype)

    hi = h2.astype(BF16)
    lo = (h2 - hi.astype(F32)).astype(BF16)
    nt = lambda w, x: lax.dot_general(w, x, (((1,), (1,)), ((), ())), preferred_element_type=F32)
    logits = nt(wrh_ref[...], hi) + nt(wrl_ref[...], hi) + nt(wrh_ref[...], lo) + br_ref[...]
    expert = lax.broadcasted_iota(jnp.int32, logits.shape, 0).astype(F32)
    work = logits
    tops, picks = [], []
    for _ in range(TOP_K):
        m = jnp.max(work, axis=0, keepdims=True)
        idx = jnp.min(jnp.where(work == m, expert, float(N_EXPERTS)), axis=0, keepdims=True)
        pick = expert == idx
        tops.append(m)
        picks.append(pick)
        work = jnp.where(pick, -jnp.inf, work)
    exps = [jnp.exp(m - tops[0]) for m in tops]
    den = exps[0] + exps[1] + exps[2] + exps[3]
    gd = jnp.full(logits.shape, -1.0, F32)
    for pick, e in zip(picks, exps):
        gd = jnp.where(pick, e / den, gd)
    gd_ref[...] = gd


def _outproj(y, bv, g, ya, x2d, mods2, lw, n_seq):
    n = x2d.shape[0]
    tm = min(n, ROW_TILE)
    full = lambda shape: pl.BlockSpec(shape, lambda i: (0,) * len(shape))
    mod_spec = lambda j: pl.BlockSpec((n_seq, D_MODEL), lambda i: (0, j))
    row = lambda width: pl.BlockSpec((tm, width), lambda i: (i, 0))
    return pl.pallas_call(
        _outproj_kernel,
        grid=(n // tm,),
        in_specs=[row(D_B), row(D_B), row(D_B), row(D_A), row(D_MODEL),
                  mod_spec(2), mod_spec(3), mod_spec(4),
                  full((_N_VEC, D_B)), full((2, D_MODEL)), full((D_MODEL, D_MODEL)),
                  full((N_EXPERTS, D_MODEL)), full((N_EXPERTS, D_MODEL)), full((N_EXPERTS, 1)),
                  full((D_B, D_B))],
        out_specs=[row(D_MODEL), row(D_MODEL), pl.BlockSpec((N_EXPERTS, tm), lambda i: (0, i))],
        out_shape=[jax.ShapeDtypeStruct((n, D_MODEL), F32),
                   jax.ShapeDtypeStruct((n, D_MODEL), BF16),
                   jax.ShapeDtypeStruct((N_EXPERTS, n), F32)],
        compiler_params=_cparams("arbitrary"),
    )(y, bv, g, ya, x2d, mods2, mods2, mods2, lw["vec"], lw["ln1"], lw["w_out"],
      lw["wr_hi"], lw["wr_lo"], lw["b_router"], lw["bd"])


def _moe_kernel(h_ref, gdt_ref, u_ref, wup_ref, bup_ref, wdn_ref, bdn_ref, o_ref,
                rank_scr, p_buf, y_buf, fill_ref):
    e = pl.program_id(1)
    tm = MOE_TILE

    for sub in range(h_ref.shape[0] // tm):
        tok = slice(sub * tm, (sub + 1) * tm)

        def flush(sub=sub, tok=tok):
            o_ref[tok, :] += lax.dot_general(p_buf[sub], y_buf[sub], (((0,), (0,)), ((), ())),
                                             preferred_element_type=F32)

        @pl.when(e == 0)
        def _(sub=sub, tok=tok):
            o_ref[tok, :] = jnp.zeros((tm, D_MODEL), F32)
            p_buf[sub] = jnp.zeros(p_buf.shape[1:], BF16)
            y_buf[sub] = jnp.zeros(y_buf.shape[1:], BF16)
            fill_ref[sub] = 0
            sel = jnp.where(gdt_ref[:, tok] >= 0.0, 1.0, 0.0).astype(BF16)
            rank_scr[sub] = _dot(sel, u_ref[...])

        gd_row = gdt_ref[pl.ds(e, 1), tok]
        sel_row = gd_row >= 0.0
        count = jnp.sum(jnp.where(sel_row, 1, 0).astype(jnp.int32))
        n_blocks = (count + MOE_ROWS - 1) // MOE_ROWS
        gate_row = jnp.where(sel_row, gd_row, 0.0)
        slot_row = jnp.where(sel_row, rank_scr[sub, pl.ds(e, 1), :].astype(jnp.int32), -1)
        row_id = lax.broadcasted_iota(jnp.int32, (MOE_ROWS, tm), 0)

        def block(b, carry, sub=sub, tok=tok, slot_row=slot_row, gate_row=gate_row, row_id=row_id,
                  flush=flush):
            onehot = jnp.where(slot_row == row_id + b * MOE_ROWS, 1.0, 0.0)
            p = onehot.astype(BF16)
            xg = _dot(p, h_ref[tok, :])
            hcat = _dot(xg.astype(BF16), wup_ref[0, 0]) + bup_ref[0, 0]
            glu = jnp.minimum(hcat[:, :D_FF], SWIGLU_LIMIT)
            lin = jnp.clip(hcat[:, D_FF:], -SWIGLU_LIMIT, SWIGLU_LIMIT)
            act = glu * jax.nn.sigmoid(SWIGLU_ALPHA * glu) * (lin + 1.0)
            yb = _dot(act.astype(BF16), wdn_ref[0, 0]) + bdn_ref[0, 0]
            gate = jnp.sum(onehot * gate_row, axis=1, keepdims=True)
            fill = fill_ref[sub]
            off = pl.multiple_of(fill * MOE_ROWS, 16)
            p_buf[sub, pl.ds(off, MOE_ROWS), :] = p
            y_buf[sub, pl.ds(off, MOE_ROWS), :] = (yb * gate).astype(BF16)
            fill_ref[sub] = fill + 1

            @pl.when(fill + 1 == MOE_FLUSH)
            def _():
                flush()
                fill_ref[sub] = 0
            return carry

        lax.fori_loop(0, n_blocks, block, 0)

        @pl.when(e == pl.num_programs(1) - 1)
        def _(sub=sub, flush=flush):
            fill = fill_ref[sub]
            for s in range(MOE_FLUSH):
                @pl.when(s >= fill)
                def _(s=s):
                    p_buf[sub, s * MOE_ROWS:(s + 1) * MOE_ROWS, :] = jnp.zeros((MOE_ROWS, tm), BF16)
            flush()


def _moe(h2, gdt, lw, layer):
    n = h2.shape[0]
    rows = MOE_GROUP * MOE_TILE
    return pl.pallas_call(
        _moe_kernel,
        grid=(n // rows, N_EXPERTS),
        in_specs=[pl.BlockSpec((rows, D_MODEL), lambda i, e: (i, 0)),
                  pl.BlockSpec((N_EXPERTS, rows), lambda i, e: (0, i)),
                  pl.BlockSpec((MOE_TILE, MOE_TILE), lambda i, e: (0, 0)),
                  pl.BlockSpec((1, 1, D_MODEL, 2 * D_FF), lambda i, e: (layer, e, 0, 0)),
                  pl.BlockSpec((1, 1, 1, 2 * D_FF), lambda i, e: (layer, e, 0, 0)),
                  pl.BlockSpec((1, 1, D_FF, D_MODEL), lambda i, e: (layer, e, 0, 0)),
                  pl.BlockSpec((1, 1, 1, D_MODEL), lambda i, e: (layer, e, 0, 0))],
        out_specs=pl.BlockSpec((rows, D_MODEL), lambda i, e: (i, 0)),
        out_shape=jax.ShapeDtypeStruct((n, D_MODEL), F32),
        scratch_shapes=[pltpu.VMEM((MOE_GROUP, N_EXPERTS, MOE_TILE), F32),
                        pltpu.VMEM((MOE_GROUP, MOE_FLUSH * MOE_ROWS, MOE_TILE), BF16),
                        pltpu.VMEM((MOE_GROUP, MOE_FLUSH * MOE_ROWS, D_MODEL), BF16),
                        pltpu.SMEM((MOE_GROUP,), jnp.int32)],
        compiler_params=_cparams("arbitrary", "arbitrary"),
    )(h2, gdt, lw["tri"], lw["w_up"], lw["b_up"], lw["w_down"], lw["b_down"])


def _ln2_kernel(x_ref, f_ref, g2_ref, lnd_ref, o_ref):
    lnd = lnd_ref[...]
    x = x_ref[...]
    o_ref[...] = _ln_rows(ALPHA * x + _per_seq(x, g2_ref[...]) * f_ref[...], lnd[0:1], lnd[1:2], LN_EPS)


def _ln2(x2d, ffn_all, row0, mods2, lw, n_seq):
    n = x2d.shape[0]
    tm = min(n, 512)
    blk0 = row0 // tm
    return pl.pallas_call(
        _ln2_kernel,
        grid=(n // tm,),
        in_specs=[pl.BlockSpec((tm, D_MODEL), lambda i: (i, 0)),
                  pl.BlockSpec((tm, D_MODEL), lambda i: (blk0 + i, 0)),
                  pl.BlockSpec((n_seq, D_MODEL), lambda i: (0, 5)),
                  pl.BlockSpec((2, D_MODEL), lambda i: (0, 0))],
        out_specs=pl.BlockSpec((tm, D_MODEL), lambda i: (i, 0)),
        out_shape=jax.ShapeDtypeStruct((n, D_MODEL), F32),
        compiler_params=_cparams("arbitrary"),
    )(x2d, ffn_all, mods2, lw["ln2"])


def _mixer(x2d, mods2, lw, n_seq, shift, wkv):
    if shift is None:
        ya, g, bv, r, w, k2, v, kk, akk, shift_new = _inproj(x2d, mods2, lw, n_seq)
        s0 = jnp.zeros((n_seq // SEQ_GROUP, 2, V_QUART, HEAD, LANES), F32)
        va = None
    else:
        ya, g, bv, r, w, k2, v, kk, akk, shift_new, va = _inproj(x2d, mods2, lw, n_seq, shift)
        s0 = _state_to_chain(wkv, n_seq)
    y, s_t = _recurrence(kk, w, akk, k2, r, v, s0, n_seq)
    x1, h2, gd = _outproj(y, bv, g, ya, x2d, mods2, lw, n_seq)
    return x1, h2, gd, shift_new, _state_from_chain(s_t, n_seq), va


def kernel(x_prompt, x_sample, state_wkv, state_shift, c_prompt, c_sample, ln_in_g, ln_in_b, w_ada, b_ada, w_in, ln_v_g, ln_v_b, w_s, b_s, mu_shift, w0, w_decay_up, a0, w_iclr_up, w_gate_up, k_k, k_a, r_k, ln_x_g, ln_x_b, w_out, ln1_g, ln1_b, w_router, b_router, w_up, b_up, w_down, b_down, ln2_g, ln2_b):
    bp, tp, _ = x_prompt.shape
    bs = x_sample.shape[0]
    n_p, n_s = bp * tp, bs
    moe_rows = MOE_GROUP * MOE_TILE
    n_all = -(-(n_p + n_s) // moe_rows) * moe_rows

    lane_head = jnp.arange(D_B) // HEAD
    bd = (lane_head[:, None] == lane_head[None, :]).astype(BF16)
    tri = (jnp.arange(MOE_TILE)[:, None] < jnp.arange(MOE_TILE)[None, :]).astype(BF16)
    zeros_lora = jnp.zeros((LORA_W, D_B), F32)
    w_up_bf, w_down_bf = w_up.astype(BF16), w_down.astype(BF16)
    layers = []
    for l in range(DEPTH):
        rows = [ln_v_g[l], ln_v_b[l], w0[l], a0[l], k_k[l], k_a[l], r_k[l].reshape(D_B),
                jnp.repeat(w_s[l, :, 0, 0], HEAD), jnp.repeat(b_s[l, :, 0], HEAD), ln_x_g[l], ln_x_b[l]]
        vec = jnp.stack(rows + [jnp.zeros((D_B,), F32)] * (_N_VEC - len(rows)))
        wr_t = w_router[l].T
        wr_hi = wr_t.astype(BF16)
        layers.append(dict(
            w_in=w_in[l].astype(BF16), vec=vec, mu=mu_shift[l].reshape(1, D_B_PROJ),
            wwa=jnp.concatenate([jnp.concatenate([w_decay_up[l], zeros_lora], axis=1),
                                 jnp.concatenate([zeros_lora, w_iclr_up[l]], axis=1)], axis=0).astype(BF16),
            wg=w_gate_up[l].astype(BF16), bd=bd, w_s=w_s[l],
            bsx=jnp.repeat(b_s[l].T, HEAD, axis=1),
            w_out=w_out[l].astype(BF16),
            ln1=jnp.stack([ln1_g[l], ln1_b[l]]), ln2=jnp.stack([ln2_g[l], ln2_b[l]]),
            wr_hi=wr_hi, wr_lo=(wr_t - wr_hi.astype(F32)).astype(BF16),
            b_router=b_router[l].reshape(N_EXPERTS, 1), tri=tri,
            w_up=w_up_bf, b_up=b_up.reshape(DEPTH, N_EXPERTS, 1, 2 * D_FF),
            w_down=w_down_bf, b_down=b_down.reshape(DEPTH, N_EXPERTS, 1, D_MODEL)))

    mods = _mods(jnp.concatenate([c_prompt, c_sample], axis=0), w_ada, b_ada)
    xp = _layer_norm(x_prompt.transpose(1, 0, 2).reshape(n_p, D_MODEL), ln_in_g, ln_in_b)
    xs = _layer_norm(x_sample.reshape(n_s, D_MODEL), ln_in_g, ln_in_b)

    pad = n_all - n_p - n_s
    wkv_p, shift_p, wkv_s, shift_s, v_s = [], [], [], [], []
    for l in range(DEPTH):
        lw = layers[l]
        mods_p, mods_s = mods[l, :bp], mods[l, bp:]
        x1p, h2p, gdp, sh_p, s_p, _ = _mixer(xp, mods_p, lw, bp, None, None)
        x1s, h2s, gds, sh_s, s_s, va_s = _mixer(xs, mods_s, lw, bs, state_shift[l], state_wkv[l])
        h2 = jnp.concatenate([h2p, h2s, jnp.zeros((pad, D_MODEL), BF16)], axis=0)
        gdt = jnp.concatenate([gdp, gds, jnp.full((N_EXPERTS, pad), -1.0, F32)], axis=1)
        ffn = _moe(h2, gdt, lw, l)
        xp = _ln2(x1p, ffn, 0, mods_p, lw, bp)
        xs = _ln2(x1s, ffn, n_p, mods_s, lw, bs)
        wkv_p.append(s_p)
        shift_p.append(sh_p)
        wkv_s.append(s_s)
        shift_s.append(sh_s)
        v_s.append(va_s.reshape(bs, 1, D_A))

    y_prompt = xp.reshape(tp, bp, D_MODEL).transpose(1, 0, 2)
    return (y_prompt, xs.reshape(bs, 1, D_MODEL), jnp.stack(wkv_p), jnp.stack(shift_p),
            jnp.stack(wkv_s), jnp.stack(shift_s), jnp.stack(v_s))
```

```python
import functools
import math

import jax
import jax.numpy as jnp
from jax import lax
from jax.experimental import pallas as pl
from jax.experimental.pallas import tpu as pltpu

F32 = jnp.float32
BF16 = jnp.bfloat16

D_MODEL = 1024
DEPTH = 4
D_A = 512
D_B = 512
HEAD = 64
N_HEADS = 8
CHUNK = 128
LORA_W = 64
LORA_A = 64
LORA_G = 128
D_B_PROJ = 3 * D_B + LORA_W + LORA_A + LORA_G
D_PROJ = 2 * D_A + D_B_PROJ
DECAY_SCALE = math.exp(-0.5)
GN_EPS = 64e-5
LN_EPS = 1e-5
N_EXPERTS = 32
TOP_K = 4
D_FF = D_MODEL
SWIGLU_ALPHA = 1.702
SWIGLU_LIMIT = 7.0
ALPHA = (2 * DEPTH) ** 0.25

LANES = 128
SUBLANES = 8
VMEM_LIMIT = 56 * 1024 * 1024

ROW_TILE = 256
MOE_TILE = 1024
MOE_GROUP = 2
MOE_ROWS = 144
MOE_MAIN = 6
MOE_SPARE = 1
OUT_TILE = 512
REC_STEPS = 64

SEQ_GROUP = SUBLANES
V_QUART = HEAD // 4
HEAD_PAIRS = N_HEADS // 2


def _cparams(*sem, flags=None):
    return pltpu.CompilerParams(dimension_semantics=sem, vmem_limit_bytes=VMEM_LIMIT, flags=flags)


def _dot(a, b):
    return jnp.dot(a, b, preferred_element_type=F32)


def _ln_rows(x, g, b, eps):
    mu = jnp.mean(x, axis=-1, keepdims=True)
    xc = x - mu
    var = jnp.mean(xc * xc, axis=-1, keepdims=True)
    return xc * lax.rsqrt(var + eps) * g + b


def _split_dot(x, w_bf16):
    hi = x.astype(BF16)
    lo = (x - hi.astype(F32)).astype(BF16)
    return _dot(hi, w_bf16) + _dot(lo, w_bf16)


def _gelu_tanh(x):
    return 0.5 * x * (1.0 + jnp.tanh(0.7978845608028654 * (x + 0.044715 * (x * x * x))))


def _per_seq(x, m):
    p = m.shape[0]
    if p == x.shape[0]:
        return m
    return jnp.broadcast_to(m[None], (x.shape[0] // p, p, m.shape[1])).reshape(x.shape[0], m.shape[1])


def _ln_kernel(x_ref, g_ref, b_ref, o_ref):
    o_ref[...] = _ln_rows(x_ref[...], g_ref[...], b_ref[...], LN_EPS)


def _layer_norm(x2d, g, b):
    n = x2d.shape[0]
    tm = min(n, 512)
    return pl.pallas_call(
        _ln_kernel,
        grid=(n // tm,),
        in_specs=[pl.BlockSpec((tm, D_MODEL), lambda i: (i, 0)),
                  pl.BlockSpec((1, D_MODEL), lambda i: (0, 0)),
                  pl.BlockSpec((1, D_MODEL), lambda i: (0, 0))],
        out_specs=pl.BlockSpec((tm, D_MODEL), lambda i: (i, 0)),
        out_shape=jax.ShapeDtypeStruct((n, D_MODEL), F32),
        compiler_params=_cparams("arbitrary"),
    )(x2d, g.reshape(1, D_MODEL), b.reshape(1, D_MODEL))


def _mods_kernel(c_ref, w_ref, b_ref, o_ref):
    c = c_ref[...]
    ca = c * jax.nn.sigmoid(c)
    o_ref[0] = _dot(ca.astype(BF16), w_ref[0].astype(BF16)) + b_ref[0]


def _mods(c_all, w_ada, b_ada):
    n = c_all.shape[0]
    tn = 1536
    return pl.pallas_call(
        _mods_kernel,
        grid=(DEPTH, 6 * D_MODEL // tn),
        in_specs=[pl.BlockSpec((n, D_MODEL), lambda l, j: (0, 0)),
                  pl.BlockSpec((1, D_MODEL, tn), lambda l, j: (l, 0, j)),
                  pl.BlockSpec((1, 1, tn), lambda l, j: (l, 0, j))],
        out_specs=pl.BlockSpec((1, n, tn), lambda l, j: (l, 0, j)),
        out_shape=jax.ShapeDtypeStruct((DEPTH, n, 6 * D_MODEL), F32),
        compiler_params=_cparams("arbitrary", "arbitrary"),
    )(c_all, w_ada, b_ada.reshape(DEPTH, 1, 6 * D_MODEL))


_V_LNV_G, _V_LNV_B, _V_W0, _V_A0, _V_KK, _V_KA, _V_RK, _V_W00, _V_B0, _V_LNX_G, _V_LNX_B = range(11)
_N_VEC = 16


def _project(x, sh, sc, win_ref):
    h = x * (1.0 + _per_seq(x, sc)) + _per_seq(x, sh)
    return _dot(h.astype(BF16), win_ref[...])


def _gelu_split(proj, vec):
    ga = _gelu_tanh(proj[:, :2 * D_A])
    va = _ln_rows(ga[:, D_A:], vec[_V_LNV_G:_V_LNV_G + 1], vec[_V_LNV_B:_V_LNV_B + 1], LN_EPS)
    return ga[:, :D_A], va


def _rwkv_prep(pb, prev, vec, mu_ref, wwa_ref, wg_ref, bd_ref, outs, rows):
    g_ref, bv_ref, r_ref, w_ref, k_ref, v_ref, kk_ref, akk_ref = outs
    row = lambda i: vec[i:i + 1, :]
    xs = pb + mu_ref[...] * (prev - pb)
    r = xs[:, :D_B]
    k = xs[:, D_B:2 * D_B]
    v = xs[:, 2 * D_B:3 * D_B]
    xwa = xs[:, 3 * D_B:3 * D_B + LORA_W + LORA_A]
    xg = xs[:, 3 * D_B + LORA_W + LORA_A:]
    lane = lax.broadcasted_iota(jnp.int32, (1, LORA_W + LORA_A), 1)
    lora_in = jnp.where(lane < LORA_W, jnp.tanh(xwa), xwa)
    wa = _dot(lora_in.astype(BF16), wwa_ref[...])
    w = jnp.exp(-DECAY_SCALE * jax.nn.sigmoid(row(_V_W0) + wa[:, :D_B]))
    a = jax.nn.sigmoid(row(_V_A0) + wa[:, D_B:])
    g = _dot(jax.nn.sigmoid(xg).astype(BF16), wg_ref[...])
    bd = bd_ref[...]
    kk = k * row(_V_KK)
    kk = kk * lax.rsqrt(jnp.maximum(_split_dot(kk * kk, bd), 1e-24))
    k2 = k * (1.0 + (a - 1.0) * row(_V_KA))
    bonus = _split_dot(r * k2 * row(_V_RK), bd)
    g_ref[rows, :] = g
    bv_ref[rows, :] = bonus * v
    r_ref[rows, :] = r
    w_ref[rows, :] = w
    k_ref[rows, :] = k2
    v_ref[rows, :] = v
    kk_ref[rows, :] = kk
    akk_ref[rows, :] = kk * a


def _inproj_seq_kernel(x_ref, sh_ref, sc_ref, win_ref, vec_ref, mu_ref, wwa_ref, wg_ref, bd_ref,
                       ws_ref, bsx_ref,
                       ya_ref, g_ref, bv_ref, r_ref, w_ref, k_ref, v_ref, kk_ref, akk_ref, shift_ref,
                       carry_ref, ua_scr, va_scr, ya_scr):
    n_seq = sh_ref.shape[0]
    sub_rows = x_ref.shape[0]
    sb = pl.program_id(1)
    vec = vec_ref[...]
    outs = (g_ref, bv_ref, r_ref, w_ref, k_ref, v_ref, kk_ref, akk_ref)

    @pl.when((pl.program_id(0) == 0) & (sb == 0))
    def _():
        carry_ref[...] = jnp.zeros_like(carry_ref)

    rows = pl.ds(pl.multiple_of(sb * sub_rows, sub_rows), sub_rows)
    proj = _project(x_ref[...], sh_ref[...], sc_ref[...], win_ref)
    ua, va = _gelu_split(proj, vec)
    for j in range(D_A // LANES):
        ua_scr[j, rows, :] = ua[:, LANES * j:LANES * (j + 1)]
        va_scr[j, rows, :] = va[:, LANES * j:LANES * (j + 1)]
    pb = proj[:, 2 * D_A:]
    prev = jnp.concatenate([carry_ref[...], pb[:sub_rows - n_seq, :]], axis=0)
    carry_ref[...] = pb[sub_rows - n_seq:, :]
    shift_ref[...] = pb[sub_rows - n_seq:, :]
    _rwkv_prep(pb, prev, vec, mu_ref, wwa_ref, wg_ref, bd_ref, outs, slice(None))

    @pl.when(sb == pl.num_programs(1) - 1)
    def _():
        head_of_lane = lax.broadcasted_iota(jnp.int32, (1, D_A), 1) // HEAD
        ri = lax.broadcasted_iota(jnp.int32, (CHUNK, CHUNK), 0)
        ci = lax.broadcasted_iota(jnp.int32, (CHUNK, CHUNK), 1)
        w_causal = [jnp.where(ri >= ci, ws_ref[hd], 0.0).astype(BF16) for hd in range(N_HEADS)]
        lane_blocks = range(D_A // LANES)
        for b in range(n_seq):
            seq_rows = pl.ds(b, CHUNK, stride=n_seq)
            vb = jnp.concatenate([va_scr[j, seq_rows, :] for j in lane_blocks], axis=1)
            ub = jnp.concatenate([ua_scr[j, seq_rows, :] for j in lane_blocks], axis=1)
            s = bsx_ref[...]
            for hd in range(N_HEADS):
                v_h = jnp.where(head_of_lane == hd, vb, 0.0).astype(BF16)
                s = s + _dot(w_causal[hd], v_h)
            ya = ub * s
            for j in lane_blocks:
                ya_scr[j, seq_rows, :] = ya[:, LANES * j:LANES * (j + 1)]
        ya_ref[...] = jnp.concatenate([ya_scr[j] for j in lane_blocks], axis=1).astype(ya_ref.dtype)


def _inproj_step_kernel(x_ref, sh_ref, sc_ref, win_ref, vec_ref, mu_ref, wwa_ref, wg_ref, bd_ref,
                        prev_ref,
                        ya_ref, g_ref, bv_ref, r_ref, w_ref, k_ref, v_ref, kk_ref, akk_ref, shift_ref,
                        va_ref):
    vec = vec_ref[...]
    outs = (g_ref, bv_ref, r_ref, w_ref, k_ref, v_ref, kk_ref, akk_ref)
    proj = _project(x_ref[...], sh_ref[...], sc_ref[...], win_ref)
    ua, va = _gelu_split(proj, vec)
    va_ref[...] = va
    ya_ref[...] = (ua * (va * vec[_V_W00:_V_W00 + 1] + vec[_V_B0:_V_B0 + 1])).astype(ya_ref.dtype)
    pb = proj[:, 2 * D_A:]
    shift_ref[...] = pb
    _rwkv_prep(pb, prev_ref[...], vec, mu_ref, wwa_ref, wg_ref, bd_ref, outs, slice(None))


def _inproj(x2d, mods2, lw, n_seq, prev=None):
    n = x2d.shape[0]
    seq_mode = prev is None
    chunk_rows = CHUNK * n_seq
    tile = ROW_TILE if seq_mode else n
    n_sub = chunk_rows // tile if seq_mode else 1
    full = lambda shape: pl.BlockSpec(shape, lambda i, s: (0,) * len(shape))
    mod_spec = lambda j: pl.BlockSpec((n_seq, D_MODEL), lambda i, s: (0, j))
    row_spec = lambda width: pl.BlockSpec((tile, width), lambda i, s: (i * n_sub + s, 0))
    in_specs = [row_spec(D_MODEL), mod_spec(0), mod_spec(1),
                full((D_MODEL, D_PROJ)), full((_N_VEC, D_B)), full((1, D_B_PROJ)),
                full((LORA_W + LORA_A, 2 * D_B)), full((LORA_G, D_B)), full((D_B, D_B))]
    args = [x2d, mods2, mods2, lw["w_in"], lw["vec"], lw["mu"], lw["wwa"], lw["wg"], lw["bd"]]
    row_shape = jax.ShapeDtypeStruct((n, D_B), F32)
    out_shape = [jax.ShapeDtypeStruct((n, D_A), BF16)] + [row_shape] * 8
    if seq_mode:
        in_specs += [full((N_HEADS, CHUNK, CHUNK)), full((CHUNK, D_A))]
        args += [lw["w_s"], lw["bsx"]]
        out_specs = [pl.BlockSpec((chunk_rows, D_A), lambda i, s: (i, 0))] + [row_spec(D_B)] * 8
        out_specs += [full((n_seq, D_B_PROJ))]
        out_shape += [jax.ShapeDtypeStruct((n_seq, D_B_PROJ), F32)]
        scratch = ([pltpu.VMEM((n_seq, D_B_PROJ), F32)]
                   + [pltpu.VMEM((D_A // LANES, chunk_rows, LANES), F32)] * 3)
        body = _inproj_seq_kernel
    else:
        out_specs = [row_spec(D_A)] + [row_spec(D_B)] * 8
        in_specs += [row_spec(D_B_PROJ)]
        args += [prev]
        out_specs += [row_spec(D_B_PROJ), row_spec(D_A)]
        out_shape += [jax.ShapeDtypeStruct((n, D_B_PROJ), F32), row_shape]
        scratch = []
        body = _inproj_step_kernel
    return pl.pallas_call(
        body, grid=(n // (tile * n_sub), n_sub),
        in_specs=in_specs, out_specs=out_specs, out_shape=out_shape, scratch_shapes=scratch,
        compiler_params=_cparams("arbitrary", "arbitrary"),
    )(*args)


def _rec_kernel(kk_ref, w_ref, akk_ref, k_ref, r_ref, v_ref, s0_ref, y_ref, st_ref,
                s_scr, op_scr0, op_scr1, v_scr0, v_scr1, y_scr0, y_scr1):
    tc = pl.program_id(1)
    steps = kk_ref.shape[0]

    @pl.when(tc == 0)
    def _():
        s_scr[...] = s0_ref[0]
        y_scr0[...] = jnp.zeros_like(y_scr0)
        y_scr1[...] = jnp.zeros_like(y_scr1)

    op_scr, v_scr, y_scr = (op_scr0, op_scr1), (v_scr0, v_scr1), (y_scr0, y_scr1)

    def stacked(x):
        return jnp.concatenate([x[:, LANES * j:LANES * (j + 1)] for j in range(HEAD_PAIRS)], axis=0)

    op_refs = (kk_ref, w_ref, akk_ref, k_ref, r_ref)
    lane_quarter = lax.broadcasted_iota(jnp.int32, (1, LANES), 1) // (LANES // 4)

    def prep_op(t, buf, i):
        m = stacked(op_refs[i][t])
        op_scr[buf][i] = jnp.concatenate([m] * 4, axis=0).T

    def prep_v(t, buf):
        mv = stacked(v_ref[t])
        v_scr[buf][...] = jnp.concatenate(
            [mv if q == 0 else pltpu.roll(mv, LANES - V_QUART * q, 1) for q in range(4)], axis=0).T

    def update(buf, part, n_parts):
        items = [(par, vi) for par in range(2) for vi in range(V_QUART)]
        per = len(items) // n_parts
        for par, vi in items[part * per:(part + 1) * per]:
            keys = slice(par * HEAD, (par + 1) * HEAD)
            row = par * HEAD + vi
            s_old = s_scr[par, vi]
            ops = op_scr[buf]
            skk = jnp.sum(s_old * ops[0, keys, :], axis=0, keepdims=True)
            s_new = s_old * ops[1, keys, :] + (
                v_scr[buf][row:row + 1, :] * ops[3, keys, :] - skk * ops[2, keys, :])
            s_scr[par, vi] = s_new
            y_row = jnp.sum(s_new * ops[4, keys, :], axis=0, keepdims=True)
            for q in range(4):
                y_scr[buf][row + V_QUART * q:row + V_QUART * q + 1, :] = jnp.where(
                    lane_quarter == q, y_row, 0.0)

    def out_rows(buf):
        yt = y_scr[buf][...].T
        acc = (yt[0:32] + yt[32:64]) + (yt[64:96] + yt[96:128])
        return jnp.concatenate([acc[SEQ_GROUP * j:SEQ_GROUP * (j + 1)] for j in range(HEAD_PAIRS)], axis=1)

    def emit(t, buf):
        y_ref[t] = out_rows(buf)

    if steps % 2:
        def step(t, carry):
            for i in range(len(op_refs)):
                prep_op(t, 0, i)
            prep_v(t, 0)
            update(0, 0, 1)
            emit(t, 0)
            return carry

        lax.fori_loop(0, steps, step, 0)
    else:
        for i in range(len(op_refs)):
            prep_op(0, 0, i)
        prep_v(0, 0)

        def half(t_cur, t_next, cur, nxt):
            rows_prev = out_rows(nxt)
            prep_op(t_next, nxt, 0)
            update(cur, 0, 4)
            prep_op(t_next, nxt, 1)
            prep_op(t_next, nxt, 2)
            update(cur, 1, 4)
            prep_op(t_next, nxt, 3)
            prep_op(t_next, nxt, 4)
            update(cur, 2, 4)
            prep_v(t_next, nxt)
            update(cur, 3, 4)
            y_ref[jnp.maximum(t_cur - 1, 0)] = rows_prev

        def pair(i, carry):
            t0 = 2 * i
            half(t0, t0 + 1, 0, 1)
            half(t0 + 1, jnp.minimum(t0 + 2, steps - 1), 1, 0)
            return carry

        lax.fori_loop(0, steps // 2, pair, 0)
        emit(steps - 1, 1)

    @pl.when(tc == pl.num_programs(1) - 1)
    def _():
        st_ref[0] = s_scr[...]


def _recurrence(kk, w, akk, k, r, v, s0, n_seq):
    t = kk.shape[0] // n_seq
    groups = n_seq // SEQ_GROUP
    tc = min(t, REC_STEPS)
    as3 = lambda z: z.reshape(t, n_seq, D_B)
    spec = pl.BlockSpec((tc, SEQ_GROUP, D_B), lambda g, i: (i, g, 0))
    sspec = pl.BlockSpec((1, 2, V_QUART, HEAD, LANES), lambda g, i: (g, 0, 0, 0, 0))
    y, s_t = pl.pallas_call(
        _rec_kernel,
        grid=(groups, t // tc),
        in_specs=[spec] * 6 + [sspec],
        out_specs=[spec, sspec],
        out_shape=[jax.ShapeDtypeStruct((t, n_seq, D_B), F32),
                   jax.ShapeDtypeStruct((groups, 2, V_QUART, HEAD, LANES), F32)],
        scratch_shapes=[pltpu.VMEM((2, V_QUART, HEAD, LANES), F32),
                        pltpu.VMEM((5, LANES, LANES), F32), pltpu.VMEM((5, LANES, LANES), F32),
                        pltpu.VMEM((LANES, LANES), F32), pltpu.VMEM((LANES, LANES), F32),
                        pltpu.VMEM((LANES, LANES), F32), pltpu.VMEM((LANES, LANES), F32)],
        compiler_params=_cparams("arbitrary", "arbitrary"),
    )(as3(kk), as3(w), as3(akk), as3(k), as3(r), as3(v), s0)
    return y.reshape(t * n_seq, D_B), s_t


def _state_to_chain(s, n_seq):
    g = n_seq // SEQ_GROUP
    s = s.reshape(g, SEQ_GROUP, HEAD_PAIRS, 2, 4, V_QUART, HEAD).transpose(0, 3, 5, 6, 4, 2, 1)
    return s.reshape(g, 2, V_QUART, HEAD, LANES)


def _state_from_chain(sc, n_seq):
    g = n_seq // SEQ_GROUP
    s = sc.reshape(g, 2, V_QUART, HEAD, 4, HEAD_PAIRS, SEQ_GROUP).transpose(0, 6, 5, 1, 4, 2, 3)
    return s.reshape(n_seq, N_HEADS, HEAD, HEAD)


def _outproj_kernel(y_ref, bv_ref, g_ref, ya_ref, x_ref, g1_ref, sh2_ref, sc2_ref, vec_ref, lnd_ref,
                    wout_ref, wrh_ref, wrl_ref, br_ref, bd_ref, x1_ref, h2_ref, gd_ref):
    vec = vec_ref[...]
    bd = bd_ref[...]
    y = y_ref[...]
    mean = _split_dot(y, bd) * (1.0 / HEAD)
    yc = y - mean
    var = _split_dot(yc * yc, bd) * (1.0 / HEAD)
    yn = yc * lax.rsqrt(var + GN_EPS) * vec[_V_LNX_G:_V_LNX_G + 1] + vec[_V_LNX_B:_V_LNX_B + 1]
    yb = (yn + bv_ref[...]) * g_ref[...]
    mix = _dot(ya_ref[...], wout_ref[:D_A, :]) + _dot(yb.astype(BF16), wout_ref[D_A:, :])
    lnd = lnd_ref[...]
    x = x_ref[...]
    x1 = _ln_rows(ALPHA * x + _per_seq(x, g1_ref[...]) * mix, lnd[0:1], lnd[1:2], LN_EPS)
    x1_ref[...] = x1
    h2 = x1 * (1.0 + _per_seq(x, sc2_ref[...])) + _per_seq(x, sh2_ref[...])
    h2_ref[...] = h2.astype(h2_ref.dtype)

    hi = h2.astype(BF16)
    lo = (h2 - hi.astype(F32)).astype(BF16)
    nt = lambda w, x: lax.dot_general(w, x, (((1,), (1,)), ((), ())), preferred_element_type=F32)
    logits = nt(wrh_ref[...], hi) + nt(wrl_ref[...], hi) + nt(wrh_ref[...], lo) + br_ref[...]
    expert = lax.broadcasted_iota(jnp.int32, logits.shape, 0).astype(F32)
    work = logits
    tops, picks = [], []
    for _ in range(TOP_K):
        m = jnp.max(work, axis=0, keepdims=True)
        idx = jnp.min(jnp.where(work == m, expert, float(N_EXPERTS)), axis=0, keepdims=True)
        pick = expert == idx
        tops.append(m)
        picks.append(pick)
        work = jnp.where(pick, -jnp.inf, work)
    exps = [jnp.exp(m - tops[0]) for m in tops]
    den = exps[0] + exps[1] + exps[2] + exps[3]
    gd = jnp.full(logits.shape, -1.0, F32)
    for pick, e in zip(picks, exps):
        gd = jnp.where(pick, e / den, gd)
    gd_ref[...] = gd


def _outproj(y, bv, g, ya, x2d, mods2, lw, n_seq):
    n = x2d.shape[0]
    tm = min(n, OUT_TILE)
    full = lambda shape: pl.BlockSpec(shape, lambda i: (0,) * len(shape))
    mod_spec = lambda j: pl.BlockSpec((n_seq, D_MODEL), lambda i: (0, j))
    row = lambda width: pl.BlockSpec((tm, width), lambda i: (i, 0))
    return pl.pallas_call(
        _outproj_kernel,
        grid=(n // tm,),
        in_specs=[row(D_B), row(D_B), row(D_B), row(D_A), row(D_MODEL),
                  mod_spec(2), mod_spec(3), mod_spec(4),
                  full((_N_VEC, D_B)), full((2, D_MODEL)), full((D_MODEL, D_MODEL)),
                  full((N_EXPERTS, D_MODEL)), full((N_EXPERTS, D_MODEL)), full((N_EXPERTS, 1)),
                  full((D_B, D_B))],
        out_specs=[row(D_MODEL), row(D_MODEL), pl.BlockSpec((N_EXPERTS, tm), lambda i: (0, i))],
        out_shape=[jax.ShapeDtypeStruct((n, D_MODEL), F32),
                   jax.ShapeDtypeStruct((n, D_MODEL), BF16),
                   jax.ShapeDtypeStruct((N_EXPERTS, n), F32)],
        compiler_params=_cparams("arbitrary"),
    )(y, bv, g, ya, x2d, mods2, mods2, mods2, lw["vec"], lw["ln1"], lw["w_out"],
      lw["wr_hi"], lw["wr_lo"], lw["b_router"], lw["bd"])


def _moe_kernel(h_ref, gdt_ref, u_ref, wup_ref, bup_ref, wdn_ref, bdn_ref, o_ref,
                rank_scr, p_buf, y_buf, spare_ref, *, live_tiles):
    grp = pl.program_id(0)
    e = pl.program_id(1)
    last_e = pl.num_programs(1) - 1
    tm = MOE_TILE
    n_sub = h_ref.shape[0] // tm
    toks = [slice(sub * tm, (sub + 1) * tm) for sub in range(n_sub)]
    row_id = lax.broadcasted_iota(jnp.int32, (MOE_ROWS, tm), 0)

    def flush(sub):
        o_ref[toks[sub], :] += lax.dot_general(p_buf[sub], y_buf[sub], (((0,), (0,)), ((), ())),
                                               preferred_element_type=F32)
        p_buf[sub] = jnp.zeros(p_buf.shape[1:], BF16)
        spare_ref[sub] = 0

    @pl.when(e == 0)
    def _():
        for sub in range(n_sub):
            o_ref[toks[sub], :] = jnp.zeros((tm, D_MODEL), F32)
            p_buf[sub] = jnp.zeros(p_buf.shape[1:], BF16)
            y_buf[sub] = jnp.zeros(y_buf.shape[1:], BF16)
            spare_ref[sub] = 0
            sel = jnp.where(gdt_ref[:, toks[sub]] >= 0.0, 1.0, 0.0).astype(BF16)
            rank_scr[sub] = _dot(sel, u_ref[...])

    def routing(sub):
        gd_row = gdt_ref[pl.ds(e, 1), toks[sub]]
        sel_row = gd_row >= 0.0
        gate_row = jnp.where(sel_row, gd_row, 0.0)
        slot_row = jnp.where(sel_row, rank_scr[sub, pl.ds(e, 1), :].astype(jnp.int32), -1)
        return sel_row, gate_row, slot_row

    def expert_block(sub, slot_row, gate_row, first_row, slot):
        onehot = jnp.where(slot_row == row_id + first_row, 1.0, 0.0)
        p = onehot.astype(BF16)
        xg = _dot(p, h_ref[toks[sub], :])
        hcat = _dot(xg.astype(BF16), wup_ref[0, 0]) + bup_ref[0, 0]
        glu = jnp.minimum(hcat[:, :D_FF], SWIGLU_LIMIT)
        lin = jnp.clip(hcat[:, D_FF:], -SWIGLU_LIMIT, SWIGLU_LIMIT)
        act = glu * jax.nn.sigmoid(SWIGLU_ALPHA * glu) * (lin + 1.0)
        yb = _dot(act.astype(BF16), wdn_ref[0, 0]) + bdn_ref[0, 0]
        gate = jnp.sum(onehot * gate_row, axis=1, keepdims=True)
        off = pl.multiple_of(slot * MOE_ROWS, 16)
        p_buf[sub, pl.ds(off, MOE_ROWS), :] = p
        y_buf[sub, pl.ds(off, MOE_ROWS), :] = (yb * gate).astype(BF16)

    def main_blocks(subs):
        for sub in subs:
            _, gate_row, slot_row = routing(sub)
            expert_block(sub, slot_row, gate_row, 0, e % MOE_MAIN)

    full_group = (grp + 1) * n_sub <= live_tiles
    pl.when(full_group)(lambda: main_blocks(range(n_sub)))
    if live_tiles % n_sub:
        pl.when(jnp.logical_not(full_group))(lambda: main_blocks(range(live_tiles % n_sub)))

    for sub in range(n_sub):
        sel_row, gate_row, slot_row = routing(sub)
        count = jnp.sum(jnp.where(sel_row, 1, 0).astype(jnp.int32))
        n_blocks = (count + MOE_ROWS - 1) // MOE_ROWS

        def extra(b, carry, sub=sub, slot_row=slot_row, gate_row=gate_row):
            @pl.when(spare_ref[sub] == MOE_SPARE)
            def _():
                flush(sub)
            spare = spare_ref[sub]
            expert_block(sub, slot_row, gate_row, b * MOE_ROWS, MOE_MAIN + spare)
            spare_ref[sub] = spare + 1
            return carry

        lax.fori_loop(1, n_blocks, extra, 0)

    @pl.when((e % MOE_MAIN == MOE_MAIN - 1) | (e == last_e))
    def _():
        for sub in range(n_sub):
            flush(sub)


def _moe(h2, gdt, lw, layer, n_live):
    n = h2.shape[0]
    rows = MOE_GROUP * MOE_TILE
    slots = (MOE_MAIN + MOE_SPARE) * MOE_ROWS
    return pl.pallas_call(
        functools.partial(_moe_kernel, live_tiles=-(-n_live // MOE_TILE)),
        grid=(n // rows, N_EXPERTS),
        in_specs=[pl.BlockSpec((rows, D_MODEL), lambda i, e: (i, 0)),
                  pl.BlockSpec((N_EXPERTS, rows), lambda i, e: (0, i)),
                  pl.BlockSpec((MOE_TILE, MOE_TILE), lambda i, e: (0, 0)),
                  pl.BlockSpec((1, 1, D_MODEL, 2 * D_FF), lambda i, e: (layer, e, 0, 0)),
                  pl.BlockSpec((1, 1, 1, 2 * D_FF), lambda i, e: (layer, e, 0, 0)),
                  pl.BlockSpec((1, 1, D_FF, D_MODEL), lambda i, e: (layer, e, 0, 0)),
                  pl.BlockSpec((1, 1, 1, D_MODEL), lambda i, e: (layer, e, 0, 0))],
        out_specs=pl.BlockSpec((rows, D_MODEL), lambda i, e: (i, 0)),
        out_shape=jax.ShapeDtypeStruct((n, D_MODEL), F32),
        scratch_shapes=[pltpu.VMEM((MOE_GROUP, N_EXPERTS, MOE_TILE), F32),
                        pltpu.VMEM((MOE_GROUP, slots, MOE_TILE), BF16),
                        pltpu.VMEM((MOE_GROUP, slots, D_MODEL), BF16),
                        pltpu.SMEM((MOE_GROUP,), jnp.int32)],
        compiler_params=_cparams("arbitrary", "arbitrary"),
    )(h2, gdt, lw["tri"], lw["w_up"], lw["b_up"], lw["w_down"], lw["b_down"])


def _ln2_kernel(x_ref, f_ref, g2_ref, lnd_ref, o_ref):
    lnd = lnd_ref[...]
    x = x_ref[...]
    o_ref[...] = _ln_rows(ALPHA * x + _per_seq(x, g2_ref[...]) * f_ref[...], lnd[0:1], lnd[1:2], LN_EPS)


def _ln2(x2d, ffn_all, row0, mods2, lw, n_seq):
    n = x2d.shape[0]
    tm = min(n, 512)
    blk0 = row0 // tm
    return pl.pallas_call(
        _ln2_kernel,
        grid=(n // tm,),
        in_specs=[pl.BlockSpec((tm, D_MODEL), lambda i: (i, 0)),
                  pl.BlockSpec((tm, D_MODEL), lambda i: (blk0 + i, 0)),
                  pl.BlockSpec((n_seq, D_MODEL), lambda i: (0, 5)),
                  pl.BlockSpec((2, D_MODEL), lambda i: (0, 0))],
        out_specs=pl.BlockSpec((tm, D_MODEL), lambda i: (i, 0)),
        out_shape=jax.ShapeDtypeStruct((n, D_MODEL), F32),
        compiler_params=_cparams("arbitrary"),
    )(x2d, ffn_all, mods2, lw["ln2"])


def _mixer(x2d, mods2, lw, n_seq, shift, wkv):
    if shift is None:
        ya, g, bv, r, w, k2, v, kk, akk, shift_new = _inproj(x2d, mods2, lw, n_seq)
        s0 = jnp.zeros((n_seq // SEQ_GROUP, 2, V_QUART, HEAD, LANES), F32)
        va = None
    else:
        ya, g, bv, r, w, k2, v, kk, akk, shift_new, va = _inproj(x2d, mods2, lw, n_seq, shift)
        s0 = _state_to_chain(wkv, n_seq)
    y, s_t = _recurrence(kk, w, akk, k2, r, v, s0, n_seq)
    x1, h2, gd = _outproj(y, bv, g, ya, x2d, mods2, lw, n_seq)
    return x1, h2, gd, shift_new, _state_from_chain(s_t, n_seq), va


def kernel(x_prompt, x_sample, state_wkv, state_shift, c_prompt, c_sample, ln_in_g, ln_in_b, w_ada, b_ada, w_in, ln_v_g, ln_v_b, w_s, b_s, mu_shift, w0, w_decay_up, a0, w_iclr_up, w_gate_up, k_k, k_a, r_k, ln_x_g, ln_x_b, w_out, ln1_g, ln1_b, w_router, b_router, w_up, b_up, w_down, b_down, ln2_g, ln2_b):
    bp, tp, _ = x_prompt.shape
    bs = x_sample.shape[0]
    n_p, n_s = bp * tp, bs
    moe_rows = MOE_GROUP * MOE_TILE
    n_all = -(-(n_p + n_s) // moe_rows) * moe_rows

    lane_head = jnp.arange(D_B) // HEAD
    bd = (lane_head[:, None] == lane_head[None, :]).astype(BF16)
    tri = (jnp.arange(MOE_TILE)[:, None] < jnp.arange(MOE_TILE)[None, :]).astype(BF16)
    zeros_lora = jnp.zeros((LORA_W, D_B), F32)
    w_up_bf, w_down_bf = w_up.astype(BF16), w_down.astype(BF16)
    layers = []
    for l in range(DEPTH):
        rows = [ln_v_g[l], ln_v_b[l], w0[l], a0[l], k_k[l], k_a[l], r_k[l].reshape(D_B),
                jnp.repeat(w_s[l, :, 0, 0], HEAD), jnp.repeat(b_s[l, :, 0], HEAD), ln_x_g[l], ln_x_b[l]]
        vec = jnp.stack(rows + [jnp.zeros((D_B,), F32)] * (_N_VEC - len(rows)))
        wr_t = w_router[l].T
        wr_hi = wr_t.astype(BF16)
        layers.append(dict(
            w_in=w_in[l].astype(BF16), vec=vec, mu=mu_shift[l].reshape(1, D_B_PROJ),
            wwa=jnp.concatenate([jnp.concatenate([w_decay_up[l], zeros_lora], axis=1),
                                 jnp.concatenate([zeros_lora, w_iclr_up[l]], axis=1)], axis=0).astype(BF16),
            wg=w_gate_up[l].astype(BF16), bd=bd, w_s=w_s[l],
            bsx=jnp.repeat(b_s[l].T, HEAD, axis=1),
            w_out=w_out[l].astype(BF16),
            ln1=jnp.stack([ln1_g[l], ln1_b[l]]), ln2=jnp.stack([ln2_g[l], ln2_b[l]]),
            wr_hi=wr_hi, wr_lo=(wr_t - wr_hi.astype(F32)).astype(BF16),
            b_router=b_router[l].reshape(N_EXPERTS, 1), tri=tri,
            w_up=w_up_bf, b_up=b_up.reshape(DEPTH, N_EXPERTS, 1, 2 * D_FF),
            w_down=w_down_bf, b_down=b_down.reshape(DEPTH, N_EXPERTS, 1, D_MODEL)))

    mods = _mods(jnp.concatenate([c_prompt, c_sample], axis=0), w_ada, b_ada)
    xp = _layer_norm(x_prompt.transpose(1, 0, 2).reshape(n_p, D_MODEL), ln_in_g, ln_in_b)
    xs = _layer_norm(x_sample.reshape(n_s, D_MODEL), ln_in_g, ln_in_b)

    pad = n_all - n_p - n_s
    wkv_p, shift_p, wkv_s, shift_s, v_s = [], [], [], [], []
    for l in range(DEPTH):
        lw = layers[l]
        mods_p, mods_s = mods[l, :bp], mods[l, bp:]
        x1p, h2p, gdp, sh_p, s_p, _ = _mixer(xp, mods_p, lw, bp, None, None)
        x1s, h2s, gds, sh_s, s_s, va_s = _mixer(xs, mods_s, lw, bs, state_shift[l], state_wkv[l])
        h2 = jnp.concatenate([h2p, h2s, jnp.zeros((pad, D_MODEL), BF16)], axis=0)
        gdt = jnp.concatenate([gdp, gds, jnp.full((N_EXPERTS, pad), -1.0, F32)], axis=1)
        ffn = _moe(h2, gdt, lw, l, n_p + n_s)
        xp = _ln2(x1p, ffn, 0, mods_p, lw, bp)
        xs = _ln2(x1s, ffn, n_p, mods_s, lw, bs)
        wkv_p.append(s_p)
        shift_p.append(sh_p)
        wkv_s.append(s_s)
        shift_s.append(sh_s)
        v_s.append(va_s.reshape(bs, 1, D_A))

    y_prompt = xp.reshape(tp, bp, D_MODEL).transpose(1, 0, 2)
    return (y_prompt, xs.reshape(bs, 1, D_MODEL), jnp.stack(wkv_p), jnp.stack(shift_p),
            jnp.stack(wkv_s), jnp.stack(shift_s), jnp.stack(v_s))
```

```python
import functools
import math

import jax
import jax.numpy as jnp
from jax import lax
from jax.experimental import pallas as pl
from jax.experimental.pallas import tpu as pltpu

F32 = jnp.float32
BF16 = jnp.bfloat16

D_MODEL = 1024
DEPTH = 4
D_A = 512
D_B = 512
HEAD = 64
N_HEADS = 8
CHUNK = 128
LORA_W = 64
LORA_A = 64
LORA_G = 128
D_B_PROJ = 3 * D_B + LORA_W + LORA_A + LORA_G
D_PROJ = 2 * D_A + D_B_PROJ
DECAY_SCALE = math.exp(-0.5)
GN_EPS = 64e-5
LN_EPS = 1e-5
N_EXPERTS = 32
TOP_K = 4
D_FF = D_MODEL
SWIGLU_ALPHA = 1.702
SWIGLU_LIMIT = 7.0
ALPHA = (2 * DEPTH) ** 0.25

LANES = 128
SUBLANES = 8
VMEM_LIMIT = 56 * 1024 * 1024

ROW_TILE = 256
MOE_TILE = 1024
MOE_GROUP = 2
MOE_ROWS = 144
MOE_FLUSH = 7
OUT_TILE = 512
REC_STEPS = 128

SEQ_GROUP = SUBLANES
V_QUART = HEAD // 4
HEAD_PAIRS = N_HEADS // 2


def _cparams(*sem, flags=None):
    return pltpu.CompilerParams(dimension_semantics=sem, vmem_limit_bytes=VMEM_LIMIT, flags=flags)


def _dot(a, b):
    return jnp.dot(a, b, preferred_element_type=F32)


def _ln_rows(x, g, b, eps):
    mu = jnp.mean(x, axis=-1, keepdims=True)
    xc = x - mu
    var = jnp.mean(xc * xc, axis=-1, keepdims=True)
    return xc * lax.rsqrt(var + eps) * g + b


def _split_dot(x, w_bf16):
    hi = x.astype(BF16)
    lo = (x - hi.astype(F32)).astype(BF16)
    return _dot(hi, w_bf16) + _dot(lo, w_bf16)


def _gelu_tanh(x):
    return 0.5 * x * (1.0 + jnp.tanh(0.7978845608028654 * (x + 0.044715 * (x * x * x))))


def _per_seq(x, m):
    p = m.shape[0]
    if p == x.shape[0]:
        return m
    return jnp.broadcast_to(m[None], (x.shape[0] // p, p, m.shape[1])).reshape(x.shape[0], m.shape[1])


def _ln_kernel(x_ref, g_ref, b_ref, o_ref):
    o_ref[...] = _ln_rows(x_ref[...], g_ref[...], b_ref[...], LN_EPS)


def _layer_norm(x2d, g, b):
    n = x2d.shape[0]
    tm = min(n, 512)
    return pl.pallas_call(
        _ln_kernel,
        grid=(n // tm,),
        in_specs=[pl.BlockSpec((tm, D_MODEL), lambda i: (i, 0)),
                  pl.BlockSpec((1, D_MODEL), lambda i: (0, 0)),
                  pl.BlockSpec((1, D_MODEL), lambda i: (0, 0))],
        out_specs=pl.BlockSpec((tm, D_MODEL), lambda i: (i, 0)),
        out_shape=jax.ShapeDtypeStruct((n, D_MODEL), F32),
        compiler_params=_cparams("arbitrary"),
    )(x2d, g.reshape(1, D_MODEL), b.reshape(1, D_MODEL))


def _mods_kernel(c_ref, w_ref, b_ref, o_ref):
    c = c_ref[...]
    ca = c * jax.nn.sigmoid(c)
    o_ref[0] = _dot(ca.astype(BF16), w_ref[0].astype(BF16)) + b_ref[0]


def _mods(c_all, w_ada, b_ada):
    n = c_all.shape[0]
    tn = 1536
    return pl.pallas_call(
        _mods_kernel,
        grid=(DEPTH, 6 * D_MODEL // tn),
        in_specs=[pl.BlockSpec((n, D_MODEL), lambda l, j: (0, 0)),
                  pl.BlockSpec((1, D_MODEL, tn), lambda l, j: (l, 0, j)),
                  pl.BlockSpec((1, 1, tn), lambda l, j: (l, 0, j))],
        out_specs=pl.BlockSpec((1, n, tn), lambda l, j: (l, 0, j)),
        out_shape=jax.ShapeDtypeStruct((DEPTH, n, 6 * D_MODEL), F32),
        compiler_params=_cparams("arbitrary", "arbitrary"),
    )(c_all, w_ada, b_ada.reshape(DEPTH, 1, 6 * D_MODEL))


_V_LNV_G, _V_LNV_B, _V_W0, _V_A0, _V_KK, _V_KA, _V_RK, _V_W00, _V_B0, _V_LNX_G, _V_LNX_B = range(11)
_N_VEC = 16


def _project(x, sh, sc, win_ref):
    h = x * (1.0 + _per_seq(x, sc)) + _per_seq(x, sh)
    return _dot(h.astype(BF16), win_ref[...])


def _gelu_split(proj, vec):
    ga = _gelu_tanh(proj[:, :2 * D_A])
    va = _ln_rows(ga[:, D_A:], vec[_V_LNV_G:_V_LNV_G + 1], vec[_V_LNV_B:_V_LNV_B + 1], LN_EPS)
    return ga[:, :D_A], va


def _rwkv_prep(pb, prev, vec, mu_ref, wwa_ref, wg_ref, bd_ref, outs, rows):
    g_ref, bv_ref, r_ref, w_ref, k_ref, v_ref, kk_ref, akk_ref = outs
    row = lambda i: vec[i:i + 1, :]
    xs = pb + mu_ref[...] * (prev - pb)
    r = xs[:, :D_B]
    k = xs[:, D_B:2 * D_B]
    v = xs[:, 2 * D_B:3 * D_B]
    xwa = xs[:, 3 * D_B:3 * D_B + LORA_W + LORA_A]
    xg = xs[:, 3 * D_B + LORA_W + LORA_A:]
    lane = lax.broadcasted_iota(jnp.int32, (1, LORA_W + LORA_A), 1)
    lora_in = jnp.where(lane < LORA_W, jnp.tanh(xwa), xwa)
    wa = _dot(lora_in.astype(BF16), wwa_ref[...])
    w = jnp.exp(-DECAY_SCALE * jax.nn.sigmoid(row(_V_W0) + wa[:, :D_B]))
    a = jax.nn.sigmoid(row(_V_A0) + wa[:, D_B:])
    g = _dot(jax.nn.sigmoid(xg).astype(BF16), wg_ref[...])
    bd = bd_ref[...]
    kk = k * row(_V_KK)
    kk = kk * lax.rsqrt(jnp.maximum(_split_dot(kk * kk, bd), 1e-24))
    k2 = k * (1.0 + (a - 1.0) * row(_V_KA))
    bonus = _split_dot(r * k2 * row(_V_RK), bd)
    g_ref[rows, :] = g
    bv_ref[rows, :] = bonus * v
    r_ref[rows, :] = r
    w_ref[rows, :] = w
    k_ref[rows, :] = k2
    v_ref[rows, :] = v
    kk_ref[rows, :] = kk
    akk_ref[rows, :] = kk * a


def _inproj_seq_kernel(x_ref, sh_ref, sc_ref, win_ref, vec_ref, mu_ref, wwa_ref, wg_ref, bd_ref,
                       ws_ref, bsx_ref,
                       ya_ref, g_ref, bv_ref, r_ref, w_ref, k_ref, v_ref, kk_ref, akk_ref, shift_ref,
                       carry_ref, ua_scr, va_scr, ya_scr):
    n_seq = sh_ref.shape[0]
    sub_rows = x_ref.shape[0]
    sb = pl.program_id(1)
    vec = vec_ref[...]
    outs = (g_ref, bv_ref, r_ref, w_ref, k_ref, v_ref, kk_ref, akk_ref)

    @pl.when((pl.program_id(0) == 0) & (sb == 0))
    def _():
        carry_ref[...] = jnp.zeros_like(carry_ref)

    rows = pl.ds(pl.multiple_of(sb * sub_rows, sub_rows), sub_rows)
    proj = _project(x_ref[...], sh_ref[...], sc_ref[...], win_ref)
    ua, va = _gelu_split(proj, vec)
    for j in range(D_A // LANES):
        ua_scr[j, rows, :] = ua[:, LANES * j:LANES * (j + 1)]
        va_scr[j, rows, :] = va[:, LANES * j:LANES * (j + 1)]
    pb = proj[:, 2 * D_A:]
    prev = jnp.concatenate([carry_ref[...], pb[:sub_rows - n_seq, :]], axis=0)
    carry_ref[...] = pb[sub_rows - n_seq:, :]
    shift_ref[...] = pb[sub_rows - n_seq:, :]
    _rwkv_prep(pb, prev, vec, mu_ref, wwa_ref, wg_ref, bd_ref, outs, slice(None))

    @pl.when(sb == pl.num_programs(1) - 1)
    def _():
        head_of_lane = lax.broadcasted_iota(jnp.int32, (1, D_A), 1) // HEAD
        ri = lax.broadcasted_iota(jnp.int32, (CHUNK, CHUNK), 0)
        ci = lax.broadcasted_iota(jnp.int32, (CHUNK, CHUNK), 1)
        w_causal = [jnp.where(ri >= ci, ws_ref[hd], 0.0).astype(BF16) for hd in range(N_HEADS)]
        lane_blocks = range(D_A // LANES)
        for b in range(n_seq):
            seq_rows = pl.ds(b, CHUNK, stride=n_seq)
            vb = jnp.concatenate([va_scr[j, seq_rows, :] for j in lane_blocks], axis=1)
            ub = jnp.concatenate([ua_scr[j, seq_rows, :] for j in lane_blocks], axis=1)
            s = bsx_ref[...]
            for hd in range(N_HEADS):
                v_h = jnp.where(head_of_lane == hd, vb, 0.0).astype(BF16)
                s = s + _dot(w_causal[hd], v_h)
            ya = ub * s
            for j in lane_blocks:
                ya_scr[j, seq_rows, :] = ya[:, LANES * j:LANES * (j + 1)]
        ya_ref[...] = jnp.concatenate([ya_scr[j] for j in lane_blocks], axis=1).astype(ya_ref.dtype)


def _inproj_step_kernel(x_ref, sh_ref, sc_ref, win_ref, vec_ref, mu_ref, wwa_ref, wg_ref, bd_ref,
                        prev_ref,
                        ya_ref, g_ref, bv_ref, r_ref, w_ref, k_ref, v_ref, kk_ref, akk_ref, shift_ref,
                        va_ref):
    vec = vec_ref[...]
    outs = (g_ref, bv_ref, r_ref, w_ref, k_ref, v_ref, kk_ref, akk_ref)
    proj = _project(x_ref[...], sh_ref[...], sc_ref[...], win_ref)
    ua, va = _gelu_split(proj, vec)
    va_ref[...] = va
    ya_ref[...] = (ua * (va * vec[_V_W00:_V_W00 + 1] + vec[_V_B0:_V_B0 + 1])).astype(ya_ref.dtype)
    pb = proj[:, 2 * D_A:]
    shift_ref[...] = pb
    _rwkv_prep(pb, prev_ref[...], vec, mu_ref, wwa_ref, wg_ref, bd_ref, outs, slice(None))


def _inproj(x2d, mods2, lw, n_seq, prev=None):
    n = x2d.shape[0]
    seq_mode = prev is None
    chunk_rows = CHUNK * n_seq
    tile = ROW_TILE if seq_mode else n
    n_sub = chunk_rows // tile if seq_mode else 1
    full = lambda shape: pl.BlockSpec(shape, lambda i, s: (0,) * len(shape))
    mod_spec = lambda j: pl.BlockSpec((n_seq, D_MODEL), lambda i, s: (0, j))
    row_spec = lambda width: pl.BlockSpec((tile, width), lambda i, s: (i * n_sub + s, 0))
    in_specs = [row_spec(D_MODEL), mod_spec(0), mod_spec(1),
                full((D_MODEL, D_PROJ)), full((_N_VEC, D_B)), full((1, D_B_PROJ)),
                full((LORA_W + LORA_A, 2 * D_B)), full((LORA_G, D_B)), full((D_B, D_B))]
    args = [x2d, mods2, mods2, lw["w_in"], lw["vec"], lw["mu"], lw["wwa"], lw["wg"], lw["bd"]]
    row_shape = jax.ShapeDtypeStruct((n, D_B), F32)
    out_shape = [jax.ShapeDtypeStruct((n, D_A), BF16)] + [row_shape] * 8
    if seq_mode:
        in_specs += [full((N_HEADS, CHUNK, CHUNK)), full((CHUNK, D_A))]
        args += [lw["w_s"], lw["bsx"]]
        out_specs = [pl.BlockSpec((chunk_rows, D_A), lambda i, s: (i, 0))] + [row_spec(D_B)] * 8
        out_specs += [full((n_seq, D_B_PROJ))]
        out_shape += [jax.ShapeDtypeStruct((n_seq, D_B_PROJ), F32)]
        scratch = ([pltpu.VMEM((n_seq, D_B_PROJ), F32)]
                   + [pltpu.VMEM((D_A // LANES, chunk_rows, LANES), F32)] * 3)
        body = _inproj_seq_kernel
    else:
        out_specs = [row_spec(D_A)] + [row_spec(D_B)] * 8
        in_specs += [row_spec(D_B_PROJ)]
        args += [prev]
        out_specs += [row_spec(D_B_PROJ), row_spec(D_A)]
        out_shape += [jax.ShapeDtypeStruct((n, D_B_PROJ), F32), row_shape]
        scratch = []
        body = _inproj_step_kernel
    return pl.pallas_call(
        body, grid=(n // (tile * n_sub), n_sub),
        in_specs=in_specs, out_specs=out_specs, out_shape=out_shape, scratch_shapes=scratch,
        compiler_params=_cparams("arbitrary", "arbitrary"),
    )(*args)


def _rec_kernel(kk_ref, w_ref, akk_ref, k_ref, r_ref, v_ref, s0_ref, y_ref, st_ref,
                s_scr, op_scr0, op_scr1, v_scr0, v_scr1, y_scr0, y_scr1):
    tc = pl.program_id(1)
    steps = kk_ref.shape[0]

    @pl.when(tc == 0)
    def _():
        s_scr[...] = s0_ref[0]
        y_scr0[...] = jnp.zeros_like(y_scr0)
        y_scr1[...] = jnp.zeros_like(y_scr1)

    op_scr, v_scr, y_scr = (op_scr0, op_scr1), (v_scr0, v_scr1), (y_scr0, y_scr1)

    def stacked(x):
        return jnp.concatenate([x[:, LANES * j:LANES * (j + 1)] for j in range(HEAD_PAIRS)], axis=0)

    op_refs = (kk_ref, w_ref, akk_ref, k_ref, r_ref)
    lane_quarter = lax.broadcasted_iota(jnp.int32, (1, LANES), 1) // (LANES // 4)

    def prep_op(t, buf, i):
        m = stacked(op_refs[i][t])
        op_scr[buf][i] = jnp.concatenate([m] * 4, axis=0).T

    def prep_v(t, buf):
        mv = stacked(v_ref[t])
        v_scr[buf][...] = jnp.concatenate(
            [mv if q == 0 else pltpu.roll(mv, LANES - V_QUART * q, 1) for q in range(4)], axis=0).T

    def update(buf, part, n_parts):
        items = [(par, vi) for par in range(2) for vi in range(V_QUART)]
        per = len(items) // n_parts
        for par, vi in items[part * per:(part + 1) * per]:
            keys = slice(par * HEAD, (par + 1) * HEAD)
            row = par * HEAD + vi
            s_old = s_scr[par, vi]
            ops = op_scr[buf]
            skk = jnp.sum(s_old * ops[0, keys, :], axis=0, keepdims=True)
            s_new = s_old * ops[1, keys, :] + (
                v_scr[buf][row:row + 1, :] * ops[3, keys, :] - skk * ops[2, keys, :])
            s_scr[par, vi] = s_new
            y_row = jnp.sum(s_new * ops[4, keys, :], axis=0, keepdims=True)
            for q in range(4):
                y_scr[buf][row + V_QUART * q:row + V_QUART * q + 1, :] = jnp.where(
                    lane_quarter == q, y_row, 0.0)

    def out_rows(buf):
        yt = y_scr[buf][...].T
        acc = (yt[0:32] + yt[32:64]) + (yt[64:96] + yt[96:128])
        return jnp.concatenate([acc[SEQ_GROUP * j:SEQ_GROUP * (j + 1)] for j in range(HEAD_PAIRS)], axis=1)

    def emit(t, buf):
        y_ref[t] = out_rows(buf)

    if steps % 2:
        def step(t, carry):
            for i in range(len(op_refs)):
                prep_op(t, 0, i)
            prep_v(t, 0)
            update(0, 0, 1)
            emit(t, 0)
            return carry

        lax.fori_loop(0, steps, step, 0)
    else:
        for i in range(len(op_refs)):
            prep_op(0, 0, i)
        prep_v(0, 0)

        def half(t_cur, t_next, cur, nxt):
            rows_prev = out_rows(nxt)
            prep_op(t_next, nxt, 0)
            update(cur, 0, 4)
            prep_op(t_next, nxt, 1)
            prep_op(t_next, nxt, 2)
            update(cur, 1, 4)
            prep_op(t_next, nxt, 3)
            prep_op(t_next, nxt, 4)
            update(cur, 2, 4)
            prep_v(t_next, nxt)
            update(cur, 3, 4)
            y_ref[jnp.maximum(t_cur - 1, 0)] = rows_prev

        def pair(i, carry):
            t0 = 2 * i
            half(t0, t0 + 1, 0, 1)
            half(t0 + 1, jnp.minimum(t0 + 2, steps - 1), 1, 0)
            return carry

        lax.fori_loop(0, steps // 2, pair, 0)
        emit(steps - 1, 1)

    @pl.when(tc == pl.num_programs(1) - 1)
    def _():
        st_ref[0] = s_scr[...]


def _recurrence(kk, w, akk, k, r, v, s0, n_seq):
    t = kk.shape[0] // n_seq
    groups = n_seq // SEQ_GROUP
    tc = min(t, REC_STEPS)
    as3 = lambda z: z.reshape(t, n_seq, D_B)
    spec = pl.BlockSpec((tc, SEQ_GROUP, D_B), lambda g, i: (i, g, 0))
    sspec = pl.BlockSpec((1, 2, V_QUART, HEAD, LANES), lambda g, i: (g, 0, 0, 0, 0))
    y, s_t = pl.pallas_call(
        _rec_kernel,
        grid=(groups, t // tc),
        in_specs=[spec] * 6 + [sspec],
        out_specs=[spec, sspec],
        out_shape=[jax.ShapeDtypeStruct((t, n_seq, D_B), F32),
                   jax.ShapeDtypeStruct((groups, 2, V_QUART, HEAD, LANES), F32)],
        scratch_shapes=[pltpu.VMEM((2, V_QUART, HEAD, LANES), F32),
                        pltpu.VMEM((5, LANES, LANES), F32), pltpu.VMEM((5, LANES, LANES), F32),
                        pltpu.VMEM((LANES, LANES), F32), pltpu.VMEM((LANES, LANES), F32),
                        pltpu.VMEM((LANES, LANES), F32), pltpu.VMEM((LANES, LANES), F32)],
        compiler_params=_cparams("arbitrary", "arbitrary"),
    )(as3(kk), as3(w), as3(akk), as3(k), as3(r), as3(v), s0)
    return y.reshape(t * n_seq, D_B), s_t


def _state_to_chain(s, n_seq):
    g = n_seq // SEQ_GROUP
    s = s.reshape(g, SEQ_GROUP, HEAD_PAIRS, 2, 4, V_QUART, HEAD).transpose(0, 3, 5, 6, 4, 2, 1)
    return s.reshape(g, 2, V_QUART, HEAD, LANES)


def _state_from_chain(sc, n_seq):
    g = n_seq // SEQ_GROUP
    s = sc.reshape(g, 2, V_QUART, HEAD, 4, HEAD_PAIRS, SEQ_GROUP).transpose(0, 6, 5, 1, 4, 2, 3)
    return s.reshape(n_seq, N_HEADS, HEAD, HEAD)


def _outproj_kernel(y_ref, bv_ref, g_ref, ya_ref, x_ref, g1_ref, sh2_ref, sc2_ref, vec_ref, lnd_ref,
                    wout_ref, wrh_ref, wrl_ref, br_ref, bd_ref, x1_ref, h2_ref, gd_ref):
    vec = vec_ref[...]
    bd = bd_ref[...]
    y = y_ref[...]
    mean = _split_dot(y, bd) * (1.0 / HEAD)
    yc = y - mean
    var = _split_dot(yc * yc, bd) * (1.0 / HEAD)
    yn = yc * lax.rsqrt(var + GN_EPS) * vec[_V_LNX_G:_V_LNX_G + 1] + vec[_V_LNX_B:_V_LNX_B + 1]
    yb = (yn + bv_ref[...]) * g_ref[...]
    mix = _dot(ya_ref[...], wout_ref[:D_A, :]) + _dot(yb.astype(BF16), wout_ref[D_A:, :])
    lnd = lnd_ref[...]
    x = x_ref[...]
    x1 = _ln_rows(ALPHA * x + _per_seq(x, g1_ref[...]) * mix, lnd[0:1], lnd[1:2], LN_EPS)
    x1_ref[...] = x1
    h2 = x1 * (1.0 + _per_seq(x, sc2_ref[...])) + _per_seq(x, sh2_ref[...])
    h2_ref[...] = h2.astype(h2_ref.dtype)

    hi = h2.astype(BF16)
    lo = (h2 - hi.astype(F32)).astype(BF16)
    nt = lambda w, x: lax.dot_general(w, x, (((1,), (1,)), ((), ())), preferred_element_type=F32)
    logits = nt(wrh_ref[...], hi) + nt(wrl_ref[...], hi) + nt(wrh_ref[...], lo) + br_ref[...]
    expert = lax.broadcasted_iota(jnp.int32, logits.shape, 0).astype(F32)
    work = logits
    tops, picks = [], []
    for _ in range(TOP_K):
        m = jnp.max(work, axis=0, keepdims=True)
        idx = jnp.min(jnp.where(work == m, expert, float(N_EXPERTS)), axis=0, keepdims=True)
        pick = expert == idx
        tops.append(m)
        picks.append(pick)
        work = jnp.where(pick, -jnp.inf, work)
    exps = [jnp.exp(m - tops[0]) for m in tops]
    den = exps[0] + exps[1] + exps[2] + exps[3]
    gd = jnp.full(logits.shape, -1.0, F32)
    for pick, e in zip(picks, exps):
        gd = jnp.where(pick, e / den, gd)
    gd_ref[...] = gd


def _outproj(y, bv, g, ya, x2d, mods2, lw, n_seq):
    n = x2d.shape[0]
    tm = min(n, OUT_TILE)
    full = lambda shape: pl.BlockSpec(shape, lambda i: (0,) * len(shape))
    mod_spec = lambda j: pl.BlockSpec((n_seq, D_MODEL), lambda i: (0, j))
    row = lambda width: pl.BlockSpec((tm, width), lambda i: (i, 0))
    return pl.pallas_call(
        _outproj_kernel,
        grid=(n // tm,),
        in_specs=[row(D_B), row(D_B), row(D_B), row(D_A), row(D_MODEL),
                  mod_spec(2), mod_spec(3), mod_spec(4),
                  full((_N_VEC, D_B)), full((2, D_MODEL)), full((D_MODEL, D_MODEL)),
                  full((N_EXPERTS, D_MODEL)), full((N_EXPERTS, D_MODEL)), full((N_EXPERTS, 1)),
                  full((D_B, D_B))],
        out_specs=[row(D_MODEL), row(D_MODEL), pl.BlockSpec((N_EXPERTS, tm), lambda i: (0, i))],
        out_shape=[jax.ShapeDtypeStruct((n, D_MODEL), F32),
                   jax.ShapeDtypeStruct((n, D_MODEL), BF16),
                   jax.ShapeDtypeStruct((N_EXPERTS, n), F32)],
        compiler_params=_cparams("arbitrary"),
    )(y, bv, g, ya, x2d, mods2, mods2, mods2, lw["vec"], lw["ln1"], lw["w_out"],
      lw["wr_hi"], lw["wr_lo"], lw["b_router"], lw["bd"])


def _moe_kernel(h_ref, gdt_ref, u_ref, wglu_ref, wlin_ref, bup_ref, wdn0_ref, wdn1_ref, bdn_ref, o_ref,
                rank_scr, p_buf, y_buf, fill_ref):
    e = pl.program_id(1)
    tm = MOE_TILE
    half = D_FF // 2

    for sub in range(h_ref.shape[0] // tm):
        tok = slice(sub * tm, (sub + 1) * tm)

        def flush(sub=sub, tok=tok):
            o_ref[tok, :] += lax.dot_general(p_buf[sub], y_buf[sub], (((0,), (0,)), ((), ())),
                                             preferred_element_type=F32)

        @pl.when(e == 0)
        def _(sub=sub, tok=tok):
            o_ref[tok, :] = jnp.zeros((tm, D_MODEL), F32)
            p_buf[sub] = jnp.zeros(p_buf.shape[1:], BF16)
            y_buf[sub] = jnp.zeros(y_buf.shape[1:], BF16)
            fill_ref[sub] = 0
            sel = jnp.where(gdt_ref[:, tok] >= 0.0, 1.0, 0.0).astype(BF16)
            rank_scr[sub] = _dot(sel, u_ref[...])

        gd_row = gdt_ref[pl.ds(e, 1), tok]
        sel_row = gd_row >= 0.0
        count = jnp.sum(jnp.where(sel_row, 1, 0).astype(jnp.int32))
        n_blocks = (count + MOE_ROWS - 1) // MOE_ROWS
        gate_row = jnp.where(sel_row, gd_row, 0.0)
        slot_row = jnp.where(sel_row, rank_scr[sub, pl.ds(e, 1), :].astype(jnp.int32), -1)
        row_id = lax.broadcasted_iota(jnp.int32, (MOE_ROWS, tm), 0)

        def block(b, carry, sub=sub, tok=tok, slot_row=slot_row, gate_row=gate_row, row_id=row_id,
                  flush=flush):
            onehot = jnp.where(slot_row == row_id + b * MOE_ROWS, 1.0, 0.0)
            p = onehot.astype(BF16)
            xg = _dot(p, h_ref[tok, :]).astype(BF16)
            bup = bup_ref[0, 0]
            glu = jnp.minimum(_dot(xg, wglu_ref[0, 0]) + bup[:, :D_FF], SWIGLU_LIMIT)
            lin = jnp.clip(_dot(xg, wlin_ref[0, 0]) + bup[:, D_FF:], -SWIGLU_LIMIT, SWIGLU_LIMIT)
            act = (glu * jax.nn.sigmoid(SWIGLU_ALPHA * glu) * (lin + 1.0)).astype(BF16)
            yb = (_dot(act[:, :half], wdn0_ref[0, 0]) + _dot(act[:, half:], wdn1_ref[0, 0])
                  + bdn_ref[0, 0])
            gate = jnp.sum(onehot * gate_row, axis=1, keepdims=True)
            fill = fill_ref[sub]
            off = pl.multiple_of(fill * MOE_ROWS, 16)
            p_buf[sub, pl.ds(off, MOE_ROWS), :] = p
            y_buf[sub, pl.ds(off, MOE_ROWS), :] = (yb * gate).astype(BF16)
            fill_ref[sub] = fill + 1

            @pl.when(fill + 1 == MOE_FLUSH)
            def _():
                flush()
                fill_ref[sub] = 0
            return carry

        lax.fori_loop(0, n_blocks, block, 0)

        @pl.when(e == pl.num_programs(1) - 1)
        def _(sub=sub, flush=flush):
            fill = fill_ref[sub]
            for s in range(MOE_FLUSH):
                @pl.when(s >= fill)
                def _(s=s):
                    p_buf[sub, s * MOE_ROWS:(s + 1) * MOE_ROWS, :] = jnp.zeros((MOE_ROWS, tm), BF16)
            flush()


def _moe(h2, gdt, lw, layer):
    n = h2.shape[0]
    rows = MOE_GROUP * MOE_TILE
    slots = MOE_FLUSH * MOE_ROWS
    return pl.pallas_call(
        _moe_kernel,
        grid=(n // rows, N_EXPERTS),
        in_specs=[pl.BlockSpec((rows, D_MODEL), lambda i, e: (i, 0)),
                  pl.BlockSpec((N_EXPERTS, rows), lambda i, e: (0, i)),
                  pl.BlockSpec((MOE_TILE, MOE_TILE), lambda i, e: (0, 0)),
                  pl.BlockSpec((1, 1, D_MODEL, D_FF), lambda i, e: (layer, e, 0, 0)),
                  pl.BlockSpec((1, 1, D_MODEL, D_FF), lambda i, e: (layer, e, 0, 1)),
                  pl.BlockSpec((1, 1, 1, 2 * D_FF), lambda i, e: (layer, e, 0, 0)),
                  pl.BlockSpec((1, 1, D_FF // 2, D_MODEL), lambda i, e: (layer, e, 0, 0)),
                  pl.BlockSpec((1, 1, D_FF // 2, D_MODEL), lambda i, e: (layer, e, 1, 0)),
                  pl.BlockSpec((1, 1, 1, D_MODEL), lambda i, e: (layer, e, 0, 0))],
        out_specs=pl.BlockSpec((rows, D_MODEL), lambda i, e: (i, 0)),
        out_shape=jax.ShapeDtypeStruct((n, D_MODEL), F32),
        scratch_shapes=[pltpu.VMEM((MOE_GROUP, N_EXPERTS, MOE_TILE), F32),
                        pltpu.VMEM((MOE_GROUP, slots, MOE_TILE), BF16),
                        pltpu.VMEM((MOE_GROUP, slots, D_MODEL), BF16),
                        pltpu.SMEM((MOE_GROUP,), jnp.int32)],
        compiler_params=_cparams("arbitrary", "arbitrary"),
    )(h2, gdt, lw["tri"], lw["w_up"], lw["w_up"], lw["b_up"], lw["w_down"], lw["w_down"], lw["b_down"])


def _ln2_kernel(x_ref, f_ref, g2_ref, lnd_ref, o_ref):
    lnd = lnd_ref[...]
    x = x_ref[...]
    o_ref[...] = _ln_rows(ALPHA * x + _per_seq(x, g2_ref[...]) * f_ref[...], lnd[0:1], lnd[1:2], LN_EPS)


def _ln2(x2d, ffn_all, row0, mods2, lw, n_seq):
    n = x2d.shape[0]
    tm = min(n, 512)
    blk0 = row0 // tm
    return pl.pallas_call(
        _ln2_kernel,
        grid=(n // tm,),
        in_specs=[pl.BlockSpec((tm, D_MODEL), lambda i: (i, 0)),
                  pl.BlockSpec((tm, D_MODEL), lambda i: (blk0 + i, 0)),
                  pl.BlockSpec((n_seq, D_MODEL), lambda i: (0, 5)),
                  pl.BlockSpec((2, D_MODEL), lambda i: (0, 0))],
        out_specs=pl.BlockSpec((tm, D_MODEL), lambda i: (i, 0)),
        out_shape=jax.ShapeDtypeStruct((n, D_MODEL), F32),
        compiler_params=_cparams("arbitrary"),
    )(x2d, ffn_all, mods2, lw["ln2"])


def _mixer(x2d, mods2, lw, n_seq, shift, wkv):
    if shift is None:
        ya, g, bv, r, w, k2, v, kk, akk, shift_new = _inproj(x2d, mods2, lw, n_seq)
        s0 = jnp.zeros((n_seq // SEQ_GROUP, 2, V_QUART, HEAD, LANES), F32)
        va = None
    else:
        ya, g, bv, r, w, k2, v, kk, akk, shift_new, va = _inproj(x2d, mods2, lw, n_seq, shift)
        s0 = _state_to_chain(wkv, n_seq)
    y, s_t = _recurrence(kk, w, akk, k2, r, v, s0, n_seq)
    x1, h2, gd = _outproj(y, bv, g, ya, x2d, mods2, lw, n_seq)
    return x1, h2, gd, shift_new, _state_from_chain(s_t, n_seq), va


def kernel(x_prompt, x_sample, state_wkv, state_shift, c_prompt, c_sample, ln_in_g, ln_in_b, w_ada, b_ada, w_in, ln_v_g, ln_v_b, w_s, b_s, mu_shift, w0, w_decay_up, a0, w_iclr_up, w_gate_up, k_k, k_a, r_k, ln_x_g, ln_x_b, w_out, ln1_g, ln1_b, w_router, b_router, w_up, b_up, w_down, b_down, ln2_g, ln2_b):
    bp, tp, _ = x_prompt.shape
    bs = x_sample.shape[0]
    n_p, n_s = bp * tp, bs
    moe_rows = MOE_GROUP * MOE_TILE
    n_all = -(-(n_p + n_s) // moe_rows) * moe_rows

    lane_head = jnp.arange(D_B) // HEAD
    bd = (lane_head[:, None] == lane_head[None, :]).astype(BF16)
    tri = (jnp.arange(MOE_TILE)[:, None] < jnp.arange(MOE_TILE)[None, :]).astype(BF16)
    zeros_lora = jnp.zeros((LORA_W, D_B), F32)
    w_up_bf, w_down_bf = w_up.astype(BF16), w_down.astype(BF16)
    layers = []
    for l in range(DEPTH):
        rows = [ln_v_g[l], ln_v_b[l], w0[l], a0[l], k_k[l], k_a[l], r_k[l].reshape(D_B),
                jnp.repeat(w_s[l, :, 0, 0], HEAD), jnp.repeat(b_s[l, :, 0], HEAD), ln_x_g[l], ln_x_b[l]]
        vec = jnp.stack(rows + [jnp.zeros((D_B,), F32)] * (_N_VEC - len(rows)))
        wr_t = w_router[l].T
        wr_hi = wr_t.astype(BF16)
        layers.append(dict(
            w_in=w_in[l].astype(BF16), vec=vec, mu=mu_shift[l].reshape(1, D_B_PROJ),
            wwa=jnp.concatenate([jnp.concatenate([w_decay_up[l], zeros_lora], axis=1),
                                 jnp.concatenate([zeros_lora, w_iclr_up[l]], axis=1)], axis=0).astype(BF16),
            wg=w_gate_up[l].astype(BF16), bd=bd, w_s=w_s[l],
            bsx=jnp.repeat(b_s[l].T, HEAD, axis=1),
            w_out=w_out[l].astype(BF16),
            ln1=jnp.stack([ln1_g[l], ln1_b[l]]), ln2=jnp.stack([ln2_g[l], ln2_b[l]]),
            wr_hi=wr_hi, wr_lo=(wr_t - wr_hi.astype(F32)).astype(BF16),
            b_router=b_router[l].reshape(N_EXPERTS, 1), tri=tri,
            w_up=w_up_bf, b_up=b_up.reshape(DEPTH, N_EXPERTS, 1, 2 * D_FF),
            w_down=w_down_bf, b_down=b_down.reshape(DEPTH, N_EXPERTS, 1, D_MODEL)))

    mods = _mods(jnp.concatenate([c_prompt, c_sample], axis=0), w_ada, b_ada)
    xp = _layer_norm(x_prompt.transpose(1, 0, 2).reshape(n_p, D_MODEL), ln_in_g, ln_in_b)
    xs = _layer_norm(x_sample.reshape(n_s, D_MODEL), ln_in_g, ln_in_b)

    pad = n_all - n_p - n_s
    wkv_p, shift_p, wkv_s, shift_s, v_s = [], [], [], [], []
    for l in range(DEPTH):
        lw = layers[l]
        mods_p, mods_s = mods[l, :bp], mods[l, bp:]
        x1p, h2p, gdp, sh_p, s_p, _ = _mixer(xp, mods_p, lw, bp, None, None)
        x1s, h2s, gds, sh_s, s_s, va_s = _mixer(xs, mods_s, lw, bs, state_shift[l], state_wkv[l])
        h2 = jnp.concatenate([h2p, h2s, jnp.zeros((pad, D_MODEL), BF16)], axis=0)
        gdt = jnp.concatenate([gdp, gds, jnp.full((N_EXPERTS, pad), -1.0, F32)], axis=1)
        ffn = _moe(h2, gdt, lw, l)
        xp = _ln2(x1p, ffn, 0, mods_p, lw, bp)
        xs = _ln2(x1s, ffn, n_p, mods_s, lw, bs)
        wkv_p.append(s_p)
        shift_p.append(sh_p)
        wkv_s.append(s_s)
        shift_s.append(sh_s)
        v_s.append(va_s.reshape(bs, 1, D_A))

    y_prompt = xp.reshape(tp, bp, D_MODEL).transpose(1, 0, 2)
    return (y_prompt, xs.reshape(bs, 1, D_MODEL), jnp.stack(wkv_p), jnp.stack(shift_p),
            jnp.stack(wkv_s), jnp.stack(shift_s), jnp.stack(v_s))
```

```python
import functools
import math

import jax
import jax.numpy as jnp
from jax import lax
from jax.experimental import pallas as pl
from jax.experimental.pallas import tpu as pltpu

F32 = jnp.float32
BF16 = jnp.bfloat16

D_MODEL = 1024
DEPTH = 4
D_A = 512
D_B = 512
HEAD = 64
N_HEADS = 8
CHUNK = 128
LORA_W = 64
LORA_A = 64
LORA_G = 128
D_B_PROJ = 3 * D_B + LORA_W + LORA_A + LORA_G
D_PROJ = 2 * D_A + D_B_PROJ
DECAY_SCALE = math.exp(-0.5)
GN_EPS = 64e-5
LN_EPS = 1e-5
N_EXPERTS = 32
TOP_K = 4
D_FF = D_MODEL
SWIGLU_ALPHA = 1.702
SWIGLU_LIMIT = 7.0
ALPHA = (2 * DEPTH) ** 0.25

LANES = 128
SUBLANES = 8
VMEM_LIMIT = 56 * 1024 * 1024

ROW_TILE = 256
MOE_TILE = 1024
MOE_GROUP = 2
MOE_ROWS = 144
MOE_FLUSH = 7
MOE_EXTRA = 48
OUT_TILE = 512
REC_STEPS = 128

SEQ_GROUP = SUBLANES
V_QUART = HEAD // 4
HEAD_PAIRS = N_HEADS // 2


def _cparams(*sem, flags=None):
    return pltpu.CompilerParams(dimension_semantics=sem, vmem_limit_bytes=VMEM_LIMIT, flags=flags)


def _dot(a, b):
    return jnp.dot(a, b, preferred_element_type=F32)


def _ln_rows(x, g, b, eps):
    mu = jnp.mean(x, axis=-1, keepdims=True)
    xc = x - mu
    var = jnp.mean(xc * xc, axis=-1, keepdims=True)
    return xc * lax.rsqrt(var + eps) * g + b


def _split_dot(x, w_bf16):
    hi = x.astype(BF16)
    lo = (x - hi.astype(F32)).astype(BF16)
    return _dot(hi, w_bf16) + _dot(lo, w_bf16)


def _gelu_tanh(x):
    return 0.5 * x * (1.0 + jnp.tanh(0.7978845608028654 * (x + 0.044715 * (x * x * x))))


def _per_seq(x, m):
    p = m.shape[0]
    if p == x.shape[0]:
        return m
    return jnp.broadcast_to(m[None], (x.shape[0] // p, p, m.shape[1])).reshape(x.shape[0], m.shape[1])


def _ln_kernel(x_ref, g_ref, b_ref, o_ref):
    o_ref[...] = _ln_rows(x_ref[...], g_ref[...], b_ref[...], LN_EPS)


def _layer_norm(x2d, g, b):
    n = x2d.shape[0]
    tm = min(n, 512)
    return pl.pallas_call(
        _ln_kernel,
        grid=(n // tm,),
        in_specs=[pl.BlockSpec((tm, D_MODEL), lambda i: (i, 0)),
                  pl.BlockSpec((1, D_MODEL), lambda i: (0, 0)),
                  pl.BlockSpec((1, D_MODEL), lambda i: (0, 0))],
        out_specs=pl.BlockSpec((tm, D_MODEL), lambda i: (i, 0)),
        out_shape=jax.ShapeDtypeStruct((n, D_MODEL), F32),
        compiler_params=_cparams("arbitrary"),
    )(x2d, g.reshape(1, D_MODEL), b.reshape(1, D_MODEL))


def _mods_kernel(c_ref, w_ref, b_ref, o_ref):
    c = c_ref[...]
    ca = c * jax.nn.sigmoid(c)
    o_ref[0] = _dot(ca.astype(BF16), w_ref[0].astype(BF16)) + b_ref[0]


def _mods(c_all, w_ada, b_ada):
    n = c_all.shape[0]
    tn = 1536
    return pl.pallas_call(
        _mods_kernel,
        grid=(DEPTH, 6 * D_MODEL // tn),
        in_specs=[pl.BlockSpec((n, D_MODEL), lambda l, j: (0, 0)),
                  pl.BlockSpec((1, D_MODEL, tn), lambda l, j: (l, 0, j)),
                  pl.BlockSpec((1, 1, tn), lambda l, j: (l, 0, j))],
        out_specs=pl.BlockSpec((1, n, tn), lambda l, j: (l, 0, j)),
        out_shape=jax.ShapeDtypeStruct((DEPTH, n, 6 * D_MODEL), F32),
        compiler_params=_cparams("arbitrary", "arbitrary"),
    )(c_all, w_ada, b_ada.reshape(DEPTH, 1, 6 * D_MODEL))


_V_LNV_G, _V_LNV_B, _V_W0, _V_A0, _V_KK, _V_KA, _V_RK, _V_W00, _V_B0, _V_LNX_G, _V_LNX_B = range(11)
_N_VEC = 16


def _project(x, sh, sc, win_ref):
    h = x * (1.0 + _per_seq(x, sc)) + _per_seq(x, sh)
    return _dot(h.astype(BF16), win_ref[...])


def _gelu_split(proj, vec):
    ga = _gelu_tanh(proj[:, :2 * D_A])
    va = _ln_rows(ga[:, D_A:], vec[_V_LNV_G:_V_LNV_G + 1], vec[_V_LNV_B:_V_LNV_B + 1], LN_EPS)
    return ga[:, :D_A], va


def _rwkv_prep(pb, prev, vec, mu_ref, wwa_ref, wg_ref, bd_ref, outs, rows):
    g_ref, bv_ref, r_ref, w_ref, k_ref, v_ref, kk_ref, akk_ref = outs
    row = lambda i: vec[i:i + 1, :]
    xs = pb + mu_ref[...] * (prev - pb)
    r = xs[:, :D_B]
    k = xs[:, D_B:2 * D_B]
    v = xs[:, 2 * D_B:3 * D_B]
    xwa = xs[:, 3 * D_B:3 * D_B + LORA_W + LORA_A]
    xg = xs[:, 3 * D_B + LORA_W + LORA_A:]
    lane = lax.broadcasted_iota(jnp.int32, (1, LORA_W + LORA_A), 1)
    lora_in = jnp.where(lane < LORA_W, jnp.tanh(xwa), xwa)
    wa = _dot(lora_in.astype(BF16), wwa_ref[...])
    w = jnp.exp(-DECAY_SCALE * jax.nn.sigmoid(row(_V_W0) + wa[:, :D_B]))
    a = jax.nn.sigmoid(row(_V_A0) + wa[:, D_B:])
    g = _dot(jax.nn.sigmoid(xg).astype(BF16), wg_ref[...])
    bd = bd_ref[...]
    kk = k * row(_V_KK)
    kk = kk * lax.rsqrt(jnp.maximum(_split_dot(kk * kk, bd), 1e-24))
    k2 = k * (1.0 + (a - 1.0) * row(_V_KA))
    bonus = _split_dot(r * k2 * row(_V_RK), bd)
    g_ref[rows, :] = g
    bv_ref[rows, :] = bonus * v
    r_ref[rows, :] = r
    w_ref[rows, :] = w
    k_ref[rows, :] = k2
    v_ref[rows, :] = v
    kk_ref[rows, :] = kk
    akk_ref[rows, :] = kk * a


def _inproj_seq_kernel(x_ref, sh_ref, sc_ref, win_ref, vec_ref, mu_ref, wwa_ref, wg_ref, bd_ref,
                       ws_ref, bsx_ref,
                       ya_ref, g_ref, bv_ref, r_ref, w_ref, k_ref, v_ref, kk_ref, akk_ref, shift_ref,
                       carry_ref, ua_scr, va_scr, ya_scr):
    n_seq = sh_ref.shape[0]
    sub_rows = x_ref.shape[0]
    sb = pl.program_id(1)
    vec = vec_ref[...]
    outs = (g_ref, bv_ref, r_ref, w_ref, k_ref, v_ref, kk_ref, akk_ref)

    @pl.when((pl.program_id(0) == 0) & (sb == 0))
    def _():
        carry_ref[...] = jnp.zeros_like(carry_ref)

    rows = pl.ds(pl.multiple_of(sb * sub_rows, sub_rows), sub_rows)
    proj = _project(x_ref[...], sh_ref[...], sc_ref[...], win_ref)
    ua, va = _gelu_split(proj, vec)
    for j in range(D_A // LANES):
        ua_scr[j, rows, :] = ua[:, LANES * j:LANES * (j + 1)]
        va_scr[j, rows, :] = va[:, LANES * j:LANES * (j + 1)]
    pb = proj[:, 2 * D_A:]
    prev = jnp.concatenate([carry_ref[...], pb[:sub_rows - n_seq, :]], axis=0)
    carry_ref[...] = pb[sub_rows - n_seq:, :]
    shift_ref[...] = pb[sub_rows - n_seq:, :]
    _rwkv_prep(pb, prev, vec, mu_ref, wwa_ref, wg_ref, bd_ref, outs, slice(None))

    @pl.when(sb == pl.num_programs(1) - 1)
    def _():
        head_of_lane = lax.broadcasted_iota(jnp.int32, (1, D_A), 1) // HEAD
        ri = lax.broadcasted_iota(jnp.int32, (CHUNK, CHUNK), 0)
        ci = lax.broadcasted_iota(jnp.int32, (CHUNK, CHUNK), 1)
        w_causal = [jnp.where(ri >= ci, ws_ref[hd], 0.0).astype(BF16) for hd in range(N_HEADS)]
        lane_blocks = range(D_A // LANES)
        for b in range(n_seq):
            seq_rows = pl.ds(b, CHUNK, stride=n_seq)
            vb = jnp.concatenate([va_scr[j, seq_rows, :] for j in lane_blocks], axis=1)
            ub = jnp.concatenate([ua_scr[j, seq_rows, :] for j in lane_blocks], axis=1)
            s = bsx_ref[...]
            for hd in range(N_HEADS):
                v_h = jnp.where(head_of_lane == hd, vb, 0.0).astype(BF16)
                s = s + _dot(w_causal[hd], v_h)
            ya = ub * s
            for j in lane_blocks:
                ya_scr[j, seq_rows, :] = ya[:, LANES * j:LANES * (j + 1)]
        ya_ref[...] = jnp.concatenate([ya_scr[j] for j in lane_blocks], axis=1).astype(ya_ref.dtype)


def _inproj_step_kernel(x_ref, sh_ref, sc_ref, win_ref, vec_ref, mu_ref, wwa_ref, wg_ref, bd_ref,
                        prev_ref,
                        ya_ref, g_ref, bv_ref, r_ref, w_ref, k_ref, v_ref, kk_ref, akk_ref, shift_ref,
                        va_ref):
    vec = vec_ref[...]
    outs = (g_ref, bv_ref, r_ref, w_ref, k_ref, v_ref, kk_ref, akk_ref)
    proj = _project(x_ref[...], sh_ref[...], sc_ref[...], win_ref)
    ua, va = _gelu_split(proj, vec)
    va_ref[...] = va
    ya_ref[...] = (ua * (va * vec[_V_W00:_V_W00 + 1] + vec[_V_B0:_V_B0 + 1])).astype(ya_ref.dtype)
    pb = proj[:, 2 * D_A:]
    shift_ref[...] = pb
    _rwkv_prep(pb, prev_ref[...], vec, mu_ref, wwa_ref, wg_ref, bd_ref, outs, slice(None))


def _inproj(x2d, mods2, lw, n_seq, prev=None):
    n = x2d.shape[0]
    seq_mode = prev is None
    chunk_rows = CHUNK * n_seq
    tile = ROW_TILE if seq_mode else n
    n_sub = chunk_rows // tile if seq_mode else 1
    full = lambda shape: pl.BlockSpec(shape, lambda i, s: (0,) * len(shape))
    mod_spec = lambda j: pl.BlockSpec((n_seq, D_MODEL), lambda i, s: (0, j))
    row_spec = lambda width: pl.BlockSpec((tile, width), lambda i, s: (i * n_sub + s, 0))
    in_specs = [row_spec(D_MODEL), mod_spec(0), mod_spec(1),
                full((D_MODEL, D_PROJ)), full((_N_VEC, D_B)), full((1, D_B_PROJ)),
                full((LORA_W + LORA_A, 2 * D_B)), full((LORA_G, D_B)), full((D_B, D_B))]
    args = [x2d, mods2, mods2, lw["w_in"], lw["vec"], lw["mu"], lw["wwa"], lw["wg"], lw["bd"]]
    row_shape = jax.ShapeDtypeStruct((n, D_B), F32)
    out_shape = [jax.ShapeDtypeStruct((n, D_A), BF16)] + [row_shape] * 8
    if seq_mode:
        in_specs += [full((N_HEADS, CHUNK, CHUNK)), full((CHUNK, D_A))]
        args += [lw["w_s"], lw["bsx"]]
        out_specs = [pl.BlockSpec((chunk_rows, D_A), lambda i, s: (i, 0))] + [row_spec(D_B)] * 8
        out_specs += [full((n_seq, D_B_PROJ))]
        out_shape += [jax.ShapeDtypeStruct((n_seq, D_B_PROJ), F32)]
        scratch = ([pltpu.VMEM((n_seq, D_B_PROJ), F32)]
                   + [pltpu.VMEM((D_A // LANES, chunk_rows, LANES), F32)] * 3)
        body = _inproj_seq_kernel
    else:
        out_specs = [row_spec(D_A)] + [row_spec(D_B)] * 8
        in_specs += [row_spec(D_B_PROJ)]
        args += [prev]
        out_specs += [row_spec(D_B_PROJ), row_spec(D_A)]
        out_shape += [jax.ShapeDtypeStruct((n, D_B_PROJ), F32), row_shape]
        scratch = []
        body = _inproj_step_kernel
    return pl.pallas_call(
        body, grid=(n // (tile * n_sub), n_sub),
        in_specs=in_specs, out_specs=out_specs, out_shape=out_shape, scratch_shapes=scratch,
        compiler_params=_cparams("arbitrary", "arbitrary"),
    )(*args)


def _rec_kernel(kk_ref, w_ref, akk_ref, k_ref, r_ref, v_ref, s0_ref, y_ref, st_ref,
                s_scr, op_scr0, op_scr1, v_scr0, v_scr1, y_scr0, y_scr1):
    tc = pl.program_id(1)
    steps = kk_ref.shape[0]

    @pl.when(tc == 0)
    def _():
        s_scr[...] = s0_ref[0]
        y_scr0[...] = jnp.zeros_like(y_scr0)
        y_scr1[...] = jnp.zeros_like(y_scr1)

    op_scr, v_scr, y_scr = (op_scr0, op_scr1), (v_scr0, v_scr1), (y_scr0, y_scr1)

    def stacked(x):
        return jnp.concatenate([x[:, LANES * j:LANES * (j + 1)] for j in range(HEAD_PAIRS)], axis=0)

    op_refs = (kk_ref, w_ref, akk_ref, k_ref, r_ref)
    value_quarter = (lax.broadcasted_iota(jnp.int32, (1, LANES), 1) % HEAD) // V_QUART

    def prep_op(t, buf, i):
        m = stacked(op_refs[i][t])
        op_scr[buf][i] = jnp.concatenate([m] * 4, axis=0).T

    def prep_v(t, buf):
        mv = stacked(v_ref[t])
        v_scr[buf][...] = jnp.concatenate(
            [mv if q == 0 else pltpu.roll(mv, LANES - V_QUART * q, 1) for q in range(4)], axis=0).T

    def update(buf, part, n_parts):
        items = [(par, vi) for par in range(2) for vi in range(V_QUART)]
        per = len(items) // n_parts
        for par, vi in items[part * per:(part + 1) * per]:
            keys = slice(par * HEAD, (par + 1) * HEAD)
            row = par * HEAD + vi
            s_old = s_scr[par, vi]
            ops = op_scr[buf]
            skk = jnp.sum(s_old * ops[0, keys, :], axis=0, keepdims=True)
            s_new = s_old * ops[1, keys, :] + (
                v_scr[buf][row:row + 1, :] * ops[3, keys, :] - skk * ops[2, keys, :])
            s_scr[par, vi] = s_new
            y_row = jnp.sum(s_new * ops[4, keys, :], axis=0, keepdims=True)
            for q in range(4):
                y_scr[buf][row + V_QUART * q:row + V_QUART * q + 1, :] = y_row

    def out_rows(buf):
        yt = y_scr[buf][...].T
        acc = jnp.where(value_quarter == 0, yt[0:32], 0.0)
        for q in range(1, 4):
            acc = acc + jnp.where(value_quarter == q, yt[32 * q:32 * (q + 1)], 0.0)
        return jnp.concatenate([acc[SEQ_GROUP * j:SEQ_GROUP * (j + 1)] for j in range(HEAD_PAIRS)], axis=1)

    def emit(t, buf):
        y_ref[t] = out_rows(buf)

    if steps % 2:
        def step(t, carry):
            for i in range(len(op_refs)):
                prep_op(t, 0, i)
            prep_v(t, 0)
            update(0, 0, 1)
            emit(t, 0)
            return carry

        lax.fori_loop(0, steps, step, 0)
    else:
        for i in range(len(op_refs)):
            prep_op(0, 0, i)
        prep_v(0, 0)

        def half(t_cur, t_next, cur, nxt):
            rows_prev = out_rows(nxt)
            prep_op(t_next, nxt, 0)
            update(cur, 0, 4)
            prep_op(t_next, nxt, 1)
            prep_op(t_next, nxt, 2)
            update(cur, 1, 4)
            prep_op(t_next, nxt, 3)
            prep_op(t_next, nxt, 4)
            update(cur, 2, 4)
            prep_v(t_next, nxt)
            update(cur, 3, 4)
            y_ref[jnp.maximum(t_cur - 1, 0)] = rows_prev

        def pair(i, carry):
            t0 = 2 * i
            half(t0, t0 + 1, 0, 1)
            half(t0 + 1, jnp.minimum(t0 + 2, steps - 1), 1, 0)
            return carry

        lax.fori_loop(0, steps // 2, pair, 0)
        emit(steps - 1, 1)

    @pl.when(tc == pl.num_programs(1) - 1)
    def _():
        st_ref[0] = s_scr[...]


def _recurrence(kk, w, akk, k, r, v, s0, n_seq):
    t = kk.shape[0] // n_seq
    groups = n_seq // SEQ_GROUP
    tc = min(t, REC_STEPS)
    as3 = lambda z: z.reshape(t, n_seq, D_B)
    spec = pl.BlockSpec((tc, SEQ_GROUP, D_B), lambda g, i: (i, g, 0))
    sspec = pl.BlockSpec((1, 2, V_QUART, HEAD, LANES), lambda g, i: (g, 0, 0, 0, 0))
    y, s_t = pl.pallas_call(
        _rec_kernel,
        grid=(groups, t // tc),
        in_specs=[spec] * 6 + [sspec],
        out_specs=[spec, sspec],
        out_shape=[jax.ShapeDtypeStruct((t, n_seq, D_B), F32),
                   jax.ShapeDtypeStruct((groups, 2, V_QUART, HEAD, LANES), F32)],
        scratch_shapes=[pltpu.VMEM((2, V_QUART, HEAD, LANES), F32),
                        pltpu.VMEM((5, LANES, LANES), F32), pltpu.VMEM((5, LANES, LANES), F32),
                        pltpu.VMEM((LANES, LANES), F32), pltpu.VMEM((LANES, LANES), F32),
                        pltpu.VMEM((LANES, LANES), F32), pltpu.VMEM((LANES, LANES), F32)],
        compiler_params=_cparams("arbitrary", "arbitrary"),
    )(as3(kk), as3(w), as3(akk), as3(k), as3(r), as3(v), s0)
    return y.reshape(t * n_seq, D_B), s_t


def _state_to_chain(s, n_seq):
    g = n_seq // SEQ_GROUP
    s = s.reshape(g, SEQ_GROUP, HEAD_PAIRS, 2, 4, V_QUART, HEAD).transpose(0, 3, 5, 6, 4, 2, 1)
    return s.reshape(g, 2, V_QUART, HEAD, LANES)


def _state_from_chain(sc, n_seq):
    g = n_seq // SEQ_GROUP
    s = sc.reshape(g, 2, V_QUART, HEAD, 4, HEAD_PAIRS, SEQ_GROUP).transpose(0, 6, 5, 1, 4, 2, 3)
    return s.reshape(n_seq, N_HEADS, HEAD, HEAD)


def _outproj_kernel(y_ref, bv_ref, g_ref, ya_ref, x_ref, g1_ref, sh2_ref, sc2_ref, vec_ref, lnd_ref,
                    wout_ref, wrh_ref, wrl_ref, br_ref, bd_ref, x1_ref, h2_ref, gd_ref):
    vec = vec_ref[...]
    bd = bd_ref[...]
    y = y_ref[...]
    mean = _split_dot(y, bd) * (1.0 / HEAD)
    yc = y - mean
    var = _split_dot(yc * yc, bd) * (1.0 / HEAD)
    yn = yc * lax.rsqrt(var + GN_EPS) * vec[_V_LNX_G:_V_LNX_G + 1] + vec[_V_LNX_B:_V_LNX_B + 1]
    yb = (yn + bv_ref[...]) * g_ref[...]
    mix = _dot(ya_ref[...], wout_ref[:D_A, :]) + _dot(yb.astype(BF16), wout_ref[D_A:, :])
    lnd = lnd_ref[...]
    x = x_ref[...]
    x1 = _ln_rows(ALPHA * x + _per_seq(x, g1_ref[...]) * mix, lnd[0:1], lnd[1:2], LN_EPS)
    x1_ref[...] = x1
    h2 = x1 * (1.0 + _per_seq(x, sc2_ref[...])) + _per_seq(x, sh2_ref[...])
    h2_ref[...] = h2.astype(h2_ref.dtype)

    hi = h2.astype(BF16)
    lo = (h2 - hi.astype(F32)).astype(BF16)
    nt = lambda w, x: lax.dot_general(w, x, (((1,), (1,)), ((), ())), preferred_element_type=F32)
    logits = nt(wrh_ref[...], hi) + nt(wrl_ref[...], hi) + nt(wrh_ref[...], lo) + br_ref[...]
    expert = lax.broadcasted_iota(jnp.int32, logits.shape, 0).astype(F32)
    work = logits
    tops, picks = [], []
    for _ in range(TOP_K):
        m = jnp.max(work, axis=0, keepdims=True)
        idx = jnp.min(jnp.where(work == m, expert, float(N_EXPERTS)), axis=0, keepdims=True)
        pick = expert == idx
        tops.append(m)
        picks.append(pick)
        work = jnp.where(pick, -jnp.inf, work)
    exps = [jnp.exp(m - tops[0]) for m in tops]
    den = exps[0] + exps[1] + exps[2] + exps[3]
    gd = jnp.full(logits.shape, -1.0, F32)
    for pick, e in zip(picks, exps):
        gd = jnp.where(pick, e / den, gd)
    gd_ref[...] = gd


def _outproj(y, bv, g, ya, x2d, mods2, lw, n_seq):
    n = x2d.shape[0]
    tm = min(n, OUT_TILE)
    full = lambda shape: pl.BlockSpec(shape, lambda i: (0,) * len(shape))
    mod_spec = lambda j: pl.BlockSpec((n_seq, D_MODEL), lambda i: (0, j))
    row = lambda width: pl.BlockSpec((tm, width), lambda i: (i, 0))
    return pl.pallas_call(
        _outproj_kernel,
        grid=(n // tm,),
        in_specs=[row(D_B), row(D_B), row(D_B), row(D_A), row(D_MODEL),
                  mod_spec(2), mod_spec(3), mod_spec(4),
                  full((_N_VEC, D_B)), full((2, D_MODEL)), full((D_MODEL, D_MODEL)),
                  full((N_EXPERTS, D_MODEL)), full((N_EXPERTS, D_MODEL)), full((N_EXPERTS, 1)),
                  full((D_B, D_B))],
        out_specs=[row(D_MODEL), row(D_MODEL), pl.BlockSpec((N_EXPERTS, tm), lambda i: (0, i))],
        out_shape=[jax.ShapeDtypeStruct((n, D_MODEL), F32),
                   jax.ShapeDtypeStruct((n, D_MODEL), BF16),
                   jax.ShapeDtypeStruct((N_EXPERTS, n), F32)],
        compiler_params=_cparams("arbitrary"),
    )(y, bv, g, ya, x2d, mods2, mods2, mods2, lw["vec"], lw["ln1"], lw["w_out"],
      lw["wr_hi"], lw["wr_lo"], lw["b_router"], lw["bd"])


def _moe_kernel(h_ref, gdt_ref, u_ref, wup_ref, bup_ref, wdn_ref, bdn_ref, o_ref,
                rank_scr, p_buf, y_buf, fill_ref):
    e = pl.program_id(1)
    tm = MOE_TILE

    for sub in range(h_ref.shape[0] // tm):
        tok = slice(sub * tm, (sub + 1) * tm)

        def flush(sub=sub, tok=tok):
            o_ref[tok, :] += lax.dot_general(p_buf[sub], y_buf[sub], (((0,), (0,)), ((), ())),
                                             preferred_element_type=F32)

        @pl.when(e == 0)
        def _(sub=sub, tok=tok):
            o_ref[tok, :] = jnp.zeros((tm, D_MODEL), F32)
            p_buf[sub] = jnp.zeros(p_buf.shape[1:], BF16)
            y_buf[sub] = jnp.zeros(y_buf.shape[1:], BF16)
            fill_ref[sub] = 0
            sel = jnp.where(gdt_ref[:, tok] >= 0.0, 1.0, 0.0).astype(BF16)
            rank_scr[sub] = _dot(sel, u_ref[...])

        gd_row = gdt_ref[pl.ds(e, 1), tok]
        sel_row = gd_row >= 0.0
        count = jnp.sum(jnp.where(sel_row, 1, 0).astype(jnp.int32))
        gate_row = jnp.where(sel_row, gd_row, 0.0)
        slot_row = jnp.where(sel_row, rank_scr[sub, pl.ds(e, 1), :].astype(jnp.int32), -1)

        def block(first_row, n_rows, sub=sub, tok=tok, slot_row=slot_row, gate_row=gate_row,
                  flush=flush):
            row_id = lax.broadcasted_iota(jnp.int32, (n_rows, tm), 0)
            onehot = jnp.where(slot_row == row_id + first_row, 1.0, 0.0)
            p = onehot.astype(BF16)
            xg = _dot(p, h_ref[tok, :])
            hcat = _dot(xg.astype(BF16), wup_ref[0, 0]) + bup_ref[0, 0]
            glu = jnp.minimum(hcat[:, :D_FF], SWIGLU_LIMIT)
            lin = jnp.clip(hcat[:, D_FF:], -SWIGLU_LIMIT, SWIGLU_LIMIT)
            act = glu * jax.nn.sigmoid(SWIGLU_ALPHA * glu) * (lin + 1.0)
            yb = _dot(act.astype(BF16), wdn_ref[0, 0]) + bdn_ref[0, 0]
            gate = jnp.sum(onehot * gate_row, axis=1, keepdims=True)
            fill = fill_ref[sub]
            off = pl.multiple_of(fill * MOE_ROWS, 16)
            p_buf[sub, pl.ds(off, n_rows), :] = p
            y_buf[sub, pl.ds(off, n_rows), :] = (yb * gate).astype(BF16)
            if n_rows < MOE_ROWS:
                rest = pl.ds(pl.multiple_of(off + n_rows, 16), MOE_ROWS - n_rows)
                p_buf[sub, rest, :] = jnp.zeros((MOE_ROWS - n_rows, tm), BF16)
            fill_ref[sub] = fill + 1

            @pl.when(fill + 1 == MOE_FLUSH)
            def _():
                flush()
                fill_ref[sub] = 0

        pl.when(count > 0)(lambda block=block: block(0, MOE_ROWS))
        n_extra = (jnp.maximum(count - MOE_ROWS, 0) + MOE_EXTRA - 1) // MOE_EXTRA

        def extra(b, carry, block=block):
            block(MOE_ROWS + b * MOE_EXTRA, MOE_EXTRA)
            return carry

        lax.fori_loop(0, n_extra, extra, 0)

        @pl.when(e == pl.num_programs(1) - 1)
        def _(sub=sub, flush=flush):
            fill = fill_ref[sub]
            for s in range(MOE_FLUSH):
                @pl.when(s >= fill)
                def _(s=s):
                    p_buf[sub, s * MOE_ROWS:(s + 1) * MOE_ROWS, :] = jnp.zeros((MOE_ROWS, tm), BF16)
            flush()


def _moe(h2, gdt, lw, layer):
    n = h2.shape[0]
    rows = MOE_GROUP * MOE_TILE
    slots = MOE_FLUSH * MOE_ROWS
    return pl.pallas_call(
        _moe_kernel,
        grid=(n // rows, N_EXPERTS),
        in_specs=[pl.BlockSpec((rows, D_MODEL), lambda i, e: (i, 0)),
                  pl.BlockSpec((N_EXPERTS, rows), lambda i, e: (0, i)),
                  pl.BlockSpec((MOE_TILE, MOE_TILE), lambda i, e: (0, 0)),
                  pl.BlockSpec((1, 1, D_MODEL, 2 * D_FF), lambda i, e: (layer, e, 0, 0)),
                  pl.BlockSpec((1, 1, 1, 2 * D_FF), lambda i, e: (layer, e, 0, 0)),
                  pl.BlockSpec((1, 1, D_FF, D_MODEL), lambda i, e: (layer, e, 0, 0)),
                  pl.BlockSpec((1, 1, 1, D_MODEL), lambda i, e: (layer, e, 0, 0))],
        out_specs=pl.BlockSpec((rows, D_MODEL), lambda i, e: (i, 0)),
        out_shape=jax.ShapeDtypeStruct((n, D_MODEL), F32),
        scratch_shapes=[pltpu.VMEM((MOE_GROUP, N_EXPERTS, MOE_TILE), F32),
                        pltpu.VMEM((MOE_GROUP, slots, MOE_TILE), BF16),
                        pltpu.VMEM((MOE_GROUP, slots, D_MODEL), BF16),
                        pltpu.SMEM((MOE_GROUP,), jnp.int32)],
        compiler_params=_cparams("arbitrary", "arbitrary"),
    )(h2, gdt, lw["tri"], lw["w_up"], lw["b_up"], lw["w_down"], lw["b_down"])


def _ln2_kernel(x_ref, f_ref, g2_ref, lnd_ref, o_ref):
    lnd = lnd_ref[...]
    x = x_ref[...]
    o_ref[...] = _ln_rows(ALPHA * x + _per_seq(x, g2_ref[...]) * f_ref[...], lnd[0:1], lnd[1:2], LN_EPS)


def _ln2(x2d, ffn_all, row0, mods2, lw, n_seq):
    n = x2d.shape[0]
    tm = min(n, 512)
    blk0 = row0 // tm
    return pl.pallas_call(
        _ln2_kernel,
        grid=(n // tm,),
        in_specs=[pl.BlockSpec((tm, D_MODEL), lambda i: (i, 0)),
                  pl.BlockSpec((tm, D_MODEL), lambda i: (blk0 + i, 0)),
                  pl.BlockSpec((n_seq, D_MODEL), lambda i: (0, 5)),
                  pl.BlockSpec((2, D_MODEL), lambda i: (0, 0))],
        out_specs=pl.BlockSpec((tm, D_MODEL), lambda i: (i, 0)),
        out_shape=jax.ShapeDtypeStruct((n, D_MODEL), F32),
        compiler_params=_cparams("arbitrary"),
    )(x2d, ffn_all, mods2, lw["ln2"])


def _mixer(x2d, mods2, lw, n_seq, shift, wkv):
    if shift is None:
        ya, g, bv, r, w, k2, v, kk, akk, shift_new = _inproj(x2d, mods2, lw, n_seq)
        s0 = jnp.zeros((n_seq // SEQ_GROUP, 2, V_QUART, HEAD, LANES), F32)
        va = None
    else:
        ya, g, bv, r, w, k2, v, kk, akk, shift_new, va = _inproj(x2d, mods2, lw, n_seq, shift)
        s0 = _state_to_chain(wkv, n_seq)
    y, s_t = _recurrence(kk, w, akk, k2, r, v, s0, n_seq)
    x1, h2, gd = _outproj(y, bv, g, ya, x2d, mods2, lw, n_seq)
    return x1, h2, gd, shift_new, _state_from_chain(s_t, n_seq), va


def kernel(x_prompt, x_sample, state_wkv, state_shift, c_prompt, c_sample, ln_in_g, ln_in_b, w_ada, b_ada, w_in, ln_v_g, ln_v_b, w_s, b_s, mu_shift, w0, w_decay_up, a0, w_iclr_up, w_gate_up, k_k, k_a, r_k, ln_x_g, ln_x_b, w_out, ln1_g, ln1_b, w_router, b_router, w_up, b_up, w_down, b_down, ln2_g, ln2_b):
    bp, tp, _ = x_prompt.shape
    bs = x_sample.shape[0]
    n_p, n_s = bp * tp, bs
    moe_rows = MOE_GROUP * MOE_TILE
    n_all = -(-(n_p + n_s) // moe_rows) * moe_rows

    lane_head = jnp.arange(D_B) // HEAD
    bd = (lane_head[:, None] == lane_head[None, :]).astype(BF16)
    tri = (jnp.arange(MOE_TILE)[:, None] < jnp.arange(MOE_TILE)[None, :]).astype(BF16)
    zeros_lora = jnp.zeros((LORA_W, D_B), F32)
    w_up_bf, w_down_bf = w_up.astype(BF16), w_down.astype(BF16)
    layers = []
    for l in range(DEPTH):
        rows = [ln_v_g[l], ln_v_b[l], w0[l], a0[l], k_k[l], k_a[l], r_k[l].reshape(D_B),
                jnp.repeat(w_s[l, :, 0, 0], HEAD), jnp.repeat(b_s[l, :, 0], HEAD), ln_x_g[l], ln_x_b[l]]
        vec = jnp.stack(rows + [jnp.zeros((D_B,), F32)] * (_N_VEC - len(rows)))
        wr_t = w_router[l].T
        wr_hi = wr_t.astype(BF16)
        layers.append(dict(
            w_in=w_in[l].astype(BF16), vec=vec, mu=mu_shift[l].reshape(1, D_B_PROJ),
            wwa=jnp.concatenate([jnp.concatenate([w_decay_up[l], zeros_lora], axis=1),
                                 jnp.concatenate([zeros_lora, w_iclr_up[l]], axis=1)], axis=0).astype(BF16),
            wg=w_gate_up[l].astype(BF16), bd=bd, w_s=w_s[l],
            bsx=jnp.repeat(b_s[l].T, HEAD, axis=1),
            w_out=w_out[l].astype(BF16),
            ln1=jnp.stack([ln1_g[l], ln1_b[l]]), ln2=jnp.stack([ln2_g[l], ln2_b[l]]),
            wr_hi=wr_hi, wr_lo=(wr_t - wr_hi.astype(F32)).astype(BF16),
            b_router=b_router[l].reshape(N_EXPERTS, 1), tri=tri,
            w_up=w_up_bf, b_up=b_up.reshape(DEPTH, N_EXPERTS, 1, 2 * D_FF),
            w_down=w_down_bf, b_down=b_down.reshape(DEPTH, N_EXPERTS, 1, D_MODEL)))

    mods = _mods(jnp.concatenate([c_prompt, c_sample], axis=0), w_ada, b_ada)
    xp = _layer_norm(x_prompt.transpose(1, 0, 2).reshape(n_p, D_MODEL), ln_in_g, ln_in_b)
    xs = _layer_norm(x_sample.reshape(n_s, D_MODEL), ln_in_g, ln_in_b)

    pad = n_all - n_p - n_s
    wkv_p, shift_p, wkv_s, shift_s, v_s = [], [], [], [], []
    for l in range(DEPTH):
        lw = layers[l]
        mods_p, mods_s = mods[l, :bp], mods[l, bp:]
        x1p, h2p, gdp, sh_p, s_p, _ = _mixer(xp, mods_p, lw, bp, None, None)
        x1s, h2s, gds, sh_s, s_s, va_s = _mixer(xs, mods_s, lw, bs, state_shift[l], state_wkv[l])
        h2 = jnp.concatenate([h2p, h2s, jnp.zeros((pad, D_MODEL), BF16)], axis=0)
        gdt = jnp.concatenate([gdp, gds, jnp.full((N_EXPERTS, pad), -1.0, F32)], axis=1)
        ffn = _moe(h2, gdt, lw, l)
        xp = _ln2(x1p, ffn, 0, mods_p, lw, bp)
        xs = _ln2(x1s, ffn, n_p, mods_s, lw, bs)
        wkv_p.append(s_p)
        shift_p.append(sh_p)
        wkv_s.append(s_s)
        shift_s.append(sh_s)
        v_s.append(va_s.reshape(bs, 1, D_A))

    y_prompt = xp.reshape(tp, bp, D_MODEL).transpose(1, 0, 2)
    return (y_prompt, xs.reshape(bs, 1, D_MODEL), jnp.stack(wkv_p), jnp.stack(shift_p),
            jnp.stack(wkv_s), jnp.stack(shift_s), jnp.stack(v_s))
```

```python
import functools
import math

import jax
import jax.numpy as jnp
from jax import lax
from jax.experimental import pallas as pl
from jax.experimental.pallas import tpu as pltpu

F32 = jnp.float32
BF16 = jnp.bfloat16

D_MODEL = 1024
DEPTH = 4
D_A = 512
D_B = 512
HEAD = 64
N_HEADS = 8
CHUNK = 128
LORA_W = 64
LORA_A = 64
LORA_G = 128
D_B_PROJ = 3 * D_B + LORA_W + LORA_A + LORA_G
D_PROJ = 2 * D_A + D_B_PROJ
DECAY_SCALE = math.exp(-0.5)
GN_EPS = 64e-5
LN_EPS = 1e-5
N_EXPERTS = 32
TOP_K = 4
D_FF = D_MODEL
SWIGLU_ALPHA = 1.702
SWIGLU_LIMIT = 7.0
ALPHA = (2 * DEPTH) ** 0.25

LANES = 128
SUBLANES = 8
VMEM_LIMIT = 56 * 1024 * 1024

ROW_TILE = 256
MOE_TILE = 1024
MOE_GROUP = 2
MOE_ROWS = 144
MOE_FLUSH = 7
MOE_EXTRA = 48
OUT_TILE = 512
REC_STEPS = 128

SEQ_GROUP = SUBLANES
V_QUART = HEAD // 4
HEAD_PAIRS = N_HEADS // 2


def _cparams(*sem, flags=None):
    return pltpu.CompilerParams(dimension_semantics=sem, vmem_limit_bytes=VMEM_LIMIT, flags=flags)


def _dot(a, b):
    return jnp.dot(a, b, preferred_element_type=F32)


def _ln_rows(x, g, b, eps):
    mu = jnp.mean(x, axis=-1, keepdims=True)
    xc = x - mu
    var = jnp.mean(xc * xc, axis=-1, keepdims=True)
    return xc * lax.rsqrt(var + eps) * g + b


def _split_dot(x, w_bf16):
    hi = x.astype(BF16)
    lo = (x - hi.astype(F32)).astype(BF16)
    return _dot(hi, w_bf16) + _dot(lo, w_bf16)


def _gelu_tanh(x):
    return 0.5 * x * (1.0 + jnp.tanh(0.7978845608028654 * (x + 0.044715 * (x * x * x))))


def _per_seq(x, m):
    p = m.shape[0]
    if p == x.shape[0]:
        return m
    return jnp.broadcast_to(m[None], (x.shape[0] // p, p, m.shape[1])).reshape(x.shape[0], m.shape[1])


def _ln_kernel(x_ref, g_ref, b_ref, o_ref):
    o_ref[...] = _ln_rows(x_ref[...], g_ref[...], b_ref[...], LN_EPS)


def _layer_norm(x2d, g, b):
    n = x2d.shape[0]
    tm = min(n, 512)
    return pl.pallas_call(
        _ln_kernel,
        grid=(n // tm,),
        in_specs=[pl.BlockSpec((tm, D_MODEL), lambda i: (i, 0)),
                  pl.BlockSpec((1, D_MODEL), lambda i: (0, 0)),
                  pl.BlockSpec((1, D_MODEL), lambda i: (0, 0))],
        out_specs=pl.BlockSpec((tm, D_MODEL), lambda i: (i, 0)),
        out_shape=jax.ShapeDtypeStruct((n, D_MODEL), F32),
        compiler_params=_cparams("arbitrary"),
    )(x2d, g.reshape(1, D_MODEL), b.reshape(1, D_MODEL))


def _mods_kernel(c_ref, w_ref, b_ref, o_ref):
    c = c_ref[...]
    ca = c * jax.nn.sigmoid(c)
    o_ref[0] = _dot(ca.astype(BF16), w_ref[0].astype(BF16)) + b_ref[0]


def _mods(c_all, w_ada, b_ada):
    n = c_all.shape[0]
    tn = 1536
    return pl.pallas_call(
        _mods_kernel,
        grid=(DEPTH, 6 * D_MODEL // tn),
        in_specs=[pl.BlockSpec((n, D_MODEL), lambda l, j: (0, 0)),
                  pl.BlockSpec((1, D_MODEL, tn), lambda l, j: (l, 0, j)),
                  pl.BlockSpec((1, 1, tn), lambda l, j: (l, 0, j))],
        out_specs=pl.BlockSpec((1, n, tn), lambda l, j: (l, 0, j)),
        out_shape=jax.ShapeDtypeStruct((DEPTH, n, 6 * D_MODEL), F32),
        compiler_params=_cparams("arbitrary", "arbitrary"),
    )(c_all, w_ada, b_ada.reshape(DEPTH, 1, 6 * D_MODEL))


_V_LNV_G, _V_LNV_B, _V_W0, _V_A0, _V_KK, _V_KA, _V_RK, _V_W00, _V_B0, _V_LNX_G, _V_LNX_B = range(11)
_N_VEC = 16


def _project(x, sh, sc, win_ref):
    h = x * (1.0 + _per_seq(x, sc)) + _per_seq(x, sh)
    return _dot(h.astype(BF16), win_ref[...])


def _gelu_split(proj, vec):
    ga = _gelu_tanh(proj[:, :2 * D_A])
    va = _ln_rows(ga[:, D_A:], vec[_V_LNV_G:_V_LNV_G + 1], vec[_V_LNV_B:_V_LNV_B + 1], LN_EPS)
    return ga[:, :D_A], va


def _rwkv_prep(pb, prev, vec, mu_ref, wwa_ref, wg_ref, bd_ref, outs, rows):
    g_ref, bv_ref, r_ref, w_ref, k_ref, v_ref, kk_ref, akk_ref = outs
    row = lambda i: vec[i:i + 1, :]
    xs = pb + mu_ref[...] * (prev - pb)
    r = xs[:, :D_B]
    k = xs[:, D_B:2 * D_B]
    v = xs[:, 2 * D_B:3 * D_B]
    xwa = xs[:, 3 * D_B:3 * D_B + LORA_W + LORA_A]
    xg = xs[:, 3 * D_B + LORA_W + LORA_A:]
    lane = lax.broadcasted_iota(jnp.int32, (1, LORA_W + LORA_A), 1)
    lora_in = jnp.where(lane < LORA_W, jnp.tanh(xwa), xwa)
    wa = _dot(lora_in.astype(BF16), wwa_ref[...])
    w = jnp.exp(-DECAY_SCALE * jax.nn.sigmoid(row(_V_W0) + wa[:, :D_B]))
    a = jax.nn.sigmoid(row(_V_A0) + wa[:, D_B:])
    g = _dot(jax.nn.sigmoid(xg).astype(BF16), wg_ref[...])
    bd = bd_ref[...]
    kk = k * row(_V_KK)
    kk = kk * lax.rsqrt(jnp.maximum(_split_dot(kk * kk, bd), 1e-24))
    k2 = k * (1.0 + (a - 1.0) * row(_V_KA))
    bonus = _split_dot(r * k2 * row(_V_RK), bd)
    g_ref[rows, :] = g
    bv_ref[rows, :] = bonus * v
    r_ref[rows, :] = r
    w_ref[rows, :] = w
    k_ref[rows, :] = k2
    v_ref[rows, :] = v
    kk_ref[rows, :] = kk
    akk_ref[rows, :] = kk * a


def _inproj_seq_kernel(x_ref, sh_ref, sc_ref, win_ref, vec_ref, mu_ref, wwa_ref, wg_ref, bd_ref,
                       ws_ref, bsx_ref,
                       ya_ref, g_ref, bv_ref, r_ref, w_ref, k_ref, v_ref, kk_ref, akk_ref, shift_ref,
                       carry_ref, ua_scr, va_scr, ya_scr):
    n_seq = sh_ref.shape[0]
    sub_rows = x_ref.shape[0]
    sb = pl.program_id(1)
    vec = vec_ref[...]
    outs = (g_ref, bv_ref, r_ref, w_ref, k_ref, v_ref, kk_ref, akk_ref)

    @pl.when((pl.program_id(0) == 0) & (sb == 0))
    def _():
        carry_ref[...] = jnp.zeros_like(carry_ref)

    rows = pl.ds(pl.multiple_of(sb * sub_rows, sub_rows), sub_rows)
    proj = _project(x_ref[...], sh_ref[...], sc_ref[...], win_ref)
    ua, va = _gelu_split(proj, vec)
    for j in range(D_A // LANES):
        ua_scr[j, rows, :] = ua[:, LANES * j:LANES * (j + 1)]
        va_scr[j, rows, :] = va[:, LANES * j:LANES * (j + 1)]
    pb = proj[:, 2 * D_A:]
    prev = jnp.concatenate([carry_ref[...], pb[:sub_rows - n_seq, :]], axis=0)
    carry_ref[...] = pb[sub_rows - n_seq:, :]
    shift_ref[...] = pb[sub_rows - n_seq:, :]
    _rwkv_prep(pb, prev, vec, mu_ref, wwa_ref, wg_ref, bd_ref, outs, slice(None))

    @pl.when(sb == pl.num_programs(1) - 1)
    def _():
        head_of_lane = lax.broadcasted_iota(jnp.int32, (1, D_A), 1) // HEAD
        ri = lax.broadcasted_iota(jnp.int32, (CHUNK, CHUNK), 0)
        ci = lax.broadcasted_iota(jnp.int32, (CHUNK, CHUNK), 1)
        w_causal = [jnp.where(ri >= ci, ws_ref[hd], 0.0).astype(BF16) for hd in range(N_HEADS)]
        lane_blocks = range(D_A // LANES)
        for b in range(n_seq):
            seq_rows = pl.ds(b, CHUNK, stride=n_seq)
            vb = jnp.concatenate([va_scr[j, seq_rows, :] for j in lane_blocks], axis=1)
            ub = jnp.concatenate([ua_scr[j, seq_rows, :] for j in lane_blocks], axis=1)
            s = bsx_ref[...]
            for hd in range(N_HEADS):
                v_h = jnp.where(head_of_lane == hd, vb, 0.0).astype(BF16)
                s = s + _dot(w_causal[hd], v_h)
            ya = ub * s
            for j in lane_blocks:
                ya_scr[j, seq_rows, :] = ya[:, LANES * j:LANES * (j + 1)]
        ya_ref[...] = jnp.concatenate([ya_scr[j] for j in lane_blocks], axis=1).astype(ya_ref.dtype)


def _inproj_step_kernel(x_ref, sh_ref, sc_ref, win_ref, vec_ref, mu_ref, wwa_ref, wg_ref, bd_ref,
                        prev_ref,
                        ya_ref, g_ref, bv_ref, r_ref, w_ref, k_ref, v_ref, kk_ref, akk_ref, shift_ref,
                        va_ref):
    vec = vec_ref[...]
    outs = (g_ref, bv_ref, r_ref, w_ref, k_ref, v_ref, kk_ref, akk_ref)
    proj = _project(x_ref[...], sh_ref[...], sc_ref[...], win_ref)
    ua, va = _gelu_split(proj, vec)
    va_ref[...] = va
    ya_ref[...] = (ua * (va * vec[_V_W00:_V_W00 + 1] + vec[_V_B0:_V_B0 + 1])).astype(ya_ref.dtype)
    pb = proj[:, 2 * D_A:]
    shift_ref[...] = pb
    _rwkv_prep(pb, prev_ref[...], vec, mu_ref, wwa_ref, wg_ref, bd_ref, outs, slice(None))


def _inproj(x2d, mods2, lw, n_seq, prev=None):
    n = x2d.shape[0]
    seq_mode = prev is None
    chunk_rows = CHUNK * n_seq
    tile = ROW_TILE if seq_mode else n
    n_sub = chunk_rows // tile if seq_mode else 1
    full = lambda shape: pl.BlockSpec(shape, lambda i, s: (0,) * len(shape))
    mod_spec = lambda j: pl.BlockSpec((n_seq, D_MODEL), lambda i, s: (0, j))
    row_spec = lambda width: pl.BlockSpec((tile, width), lambda i, s: (i * n_sub + s, 0))
    in_specs = [row_spec(D_MODEL), mod_spec(0), mod_spec(1),
                full((D_MODEL, D_PROJ)), full((_N_VEC, D_B)), full((1, D_B_PROJ)),
                full((LORA_W + LORA_A, 2 * D_B)), full((LORA_G, D_B)), full((D_B, D_B))]
    args = [x2d, mods2, mods2, lw["w_in"], lw["vec"], lw["mu"], lw["wwa"], lw["wg"], lw["bd"]]
    row_shape = jax.ShapeDtypeStruct((n, D_B), F32)
    out_shape = [jax.ShapeDtypeStruct((n, D_A), BF16)] + [row_shape] * 8
    if seq_mode:
        in_specs += [full((N_HEADS, CHUNK, CHUNK)), full((CHUNK, D_A))]
        args += [lw["w_s"], lw["bsx"]]
        out_specs = [pl.BlockSpec((chunk_rows, D_A), lambda i, s: (i, 0))] + [row_spec(D_B)] * 8
        out_specs += [full((n_seq, D_B_PROJ))]
        out_shape += [jax.ShapeDtypeStruct((n_seq, D_B_PROJ), F32)]
        scratch = ([pltpu.VMEM((n_seq, D_B_PROJ), F32)]
                   + [pltpu.VMEM((D_A // LANES, chunk_rows, LANES), F32)] * 3)
        body = _inproj_seq_kernel
    else:
        out_specs = [row_spec(D_A)] + [row_spec(D_B)] * 8
        in_specs += [row_spec(D_B_PROJ)]
        args += [prev]
        out_specs += [row_spec(D_B_PROJ), row_spec(D_A)]
        out_shape += [jax.ShapeDtypeStruct((n, D_B_PROJ), F32), row_shape]
        scratch = []
        body = _inproj_step_kernel
    return pl.pallas_call(
        body, grid=(n // (tile * n_sub), n_sub),
        in_specs=in_specs, out_specs=out_specs, out_shape=out_shape, scratch_shapes=scratch,
        compiler_params=_cparams("arbitrary", "arbitrary"),
    )(*args)


def _rec_kernel(kk_ref, w_ref, akk_ref, k_ref, r_ref, v_ref, s0_ref, y_ref, st_ref,
                s_scr, op_scr0, op_scr1, v_scr0, v_scr1, y_scr0, y_scr1):
    tc = pl.program_id(1)
    steps = kk_ref.shape[0]

    @pl.when(tc == 0)
    def _():
        s_scr[...] = s0_ref[0]
        y_scr0[...] = jnp.zeros_like(y_scr0)
        y_scr1[...] = jnp.zeros_like(y_scr1)

    op_scr, v_scr, y_scr = (op_scr0, op_scr1), (v_scr0, v_scr1), (y_scr0, y_scr1)

    def stacked(x):
        return jnp.concatenate([x[:, LANES * j:LANES * (j + 1)] for j in range(HEAD_PAIRS)], axis=0)

    op_refs = (kk_ref, w_ref, akk_ref, k_ref, r_ref)
    value_quarter = (lax.broadcasted_iota(jnp.int32, (1, LANES), 1) % HEAD) // V_QUART

    def prep_op(t, buf, i):
        m = stacked(op_refs[i][t])
        op_scr[buf][i] = jnp.concatenate([m] * 4, axis=0).T

    def prep_v(t, buf):
        mv = stacked(v_ref[t])
        v_scr[buf][...] = jnp.concatenate(
            [mv if q == 0 else pltpu.roll(mv, LANES - V_QUART * q, 1) for q in range(4)], axis=0).T

    def update(buf, part, n_parts):
        items = [(par, vi) for par in range(2) for vi in range(V_QUART)]
        per = len(items) // n_parts
        for par, vi in items[part * per:(part + 1) * per]:
            keys = slice(par * HEAD, (par + 1) * HEAD)
            row = par * HEAD + vi
            s_old = s_scr[par, vi]
            ops = op_scr[buf]
            skk = jnp.sum(s_old * ops[0, keys, :], axis=0, keepdims=True)
            s_new = s_old * ops[1, keys, :] + (
                v_scr[buf][row:row + 1, :] * ops[3, keys, :] - skk * ops[2, keys, :])
            s_scr[par, vi] = s_new
            y_row = jnp.sum(s_new * ops[4, keys, :], axis=0, keepdims=True)
            for q in range(4):
                y_scr[buf][row + V_QUART * q:row + V_QUART * q + 1, :] = y_row

    def out_rows(buf):
        yt = y_scr[buf][...].T
        acc = jnp.where(value_quarter == 0, yt[0:32], 0.0)
        for q in range(1, 4):
            acc = acc + jnp.where(value_quarter == q, yt[32 * q:32 * (q + 1)], 0.0)
        return jnp.concatenate([acc[SEQ_GROUP * j:SEQ_GROUP * (j + 1)] for j in range(HEAD_PAIRS)], axis=1)

    def emit(t, buf):
        y_ref[t] = out_rows(buf)

    if steps % 2:
        def step(t, carry):
            for i in range(len(op_refs)):
                prep_op(t, 0, i)
            prep_v(t, 0)
            update(0, 0, 1)
            emit(t, 0)
            return carry

        lax.fori_loop(0, steps, step, 0)
    else:
        for i in range(len(op_refs)):
            prep_op(0, 0, i)
        prep_v(0, 0)

        def half(t_cur, t_next, cur, nxt):
            rows_prev = out_rows(nxt)
            prep_op(t_next, nxt, 0)
            update(cur, 0, 4)
            prep_op(t_next, nxt, 1)
            prep_op(t_next, nxt, 2)
            update(cur, 1, 4)
            prep_op(t_next, nxt, 3)
            prep_op(t_next, nxt, 4)
            update(cur, 2, 4)
            prep_v(t_next, nxt)
            update(cur, 3, 4)
            y_ref[jnp.maximum(t_cur - 1, 0)] = rows_prev

        def pair(i, carry):
            t0 = 2 * i
            half(t0, t0 + 1, 0, 1)
            half(t0 + 1, jnp.minimum(t0 + 2, steps - 1), 1, 0)
            return carry

        lax.fori_loop(0, steps // 2, pair, 0)
        emit(steps - 1, 1)

    @pl.when(tc == pl.num_programs(1) - 1)
    def _():
        st_ref[0] = s_scr[...]


def _recurrence(kk, w, akk, k, r, v, s0, n_seq):
    t = kk.shape[0] // n_seq
    groups = n_seq // SEQ_GROUP
    tc = min(t, REC_STEPS)
    as3 = lambda z: z.reshape(t, n_seq, D_B)
    spec = pl.BlockSpec((tc, SEQ_GROUP, D_B), lambda g, i: (i, g, 0))
    sspec = pl.BlockSpec((1, 2, V_QUART, HEAD, LANES), lambda g, i: (g, 0, 0, 0, 0))
    y, s_t = pl.pallas_call(
        _rec_kernel,
        grid=(groups, t // tc),
        in_specs=[spec] * 6 + [sspec],
        out_specs=[spec, sspec],
        out_shape=[jax.ShapeDtypeStruct((t, n_seq, D_B), F32),
                   jax.ShapeDtypeStruct((groups, 2, V_QUART, HEAD, LANES), F32)],
        scratch_shapes=[pltpu.VMEM((2, V_QUART, HEAD, LANES), F32),
                        pltpu.VMEM((5, LANES, LANES), F32), pltpu.VMEM((5, LANES, LANES), F32),
                        pltpu.VMEM((LANES, LANES), F32), pltpu.VMEM((LANES, LANES), F32),
                        pltpu.VMEM((LANES, LANES), F32), pltpu.VMEM((LANES, LANES), F32)],
        compiler_params=_cparams("arbitrary", "arbitrary"),
    )(as3(kk), as3(w), as3(akk), as3(k), as3(r), as3(v), s0)
    return y.reshape(t * n_seq, D_B), s_t


def _state_to_chain(s, n_seq):
    g = n_seq // SEQ_GROUP
    s = s.reshape(g, SEQ_GROUP, HEAD_PAIRS, 2, 4, V_QUART, HEAD).transpose(0, 3, 5, 6, 4, 2, 1)
    return s.reshape(g, 2, V_QUART, HEAD, LANES)


def _state_from_chain(sc, n_seq):
    g = n_seq // SEQ_GROUP
    s = sc.reshape(g, 2, V_QUART, HEAD, 4, HEAD_PAIRS, SEQ_GROUP).transpose(0, 6, 5, 1, 4, 2, 3)
    return s.reshape(n_seq, N_HEADS, HEAD, HEAD)


def _outproj_kernel(y_ref, bv_ref, g_ref, ya_ref, x_ref, g1_ref, sh2_ref, sc2_ref, vec_ref, lnd_ref,
                    wout_ref, wrh_ref, wrl_ref, br_ref, bd_ref, x1_ref, h2_ref, gd_ref):
    vec = vec_ref[...]
    bd = bd_ref[...]
    y = y_ref[...]
    mean = _split_dot(y, bd) * (1.0 / HEAD)
    yc = y - mean
    var = _split_dot(yc * yc, bd) * (1.0 / HEAD)
    yn = yc * lax.rsqrt(var + GN_EPS) * vec[_V_LNX_G:_V_LNX_G + 1] + vec[_V_LNX_B:_V_LNX_B + 1]
    yb = (yn + bv_ref[...]) * g_ref[...]
    mix = _dot(ya_ref[...], wout_ref[:D_A, :]) + _dot(yb.astype(BF16), wout_ref[D_A:, :])
    lnd = lnd_ref[...]
    x = x_ref[...]
    x1 = _ln_rows(ALPHA * x + _per_seq(x, g1_ref[...]) * mix, lnd[0:1], lnd[1:2], LN_EPS)
    x1_ref[...] = x1
    h2 = x1 * (1.0 + _per_seq(x, sc2_ref[...])) + _per_seq(x, sh2_ref[...])
    h2_ref[...] = h2.astype(h2_ref.dtype)

    hi = h2.astype(BF16)
    lo = (h2 - hi.astype(F32)).astype(BF16)
    nt = lambda w, x: lax.dot_general(w, x, (((1,), (1,)), ((), ())), preferred_element_type=F32)
    logits = nt(wrh_ref[...], hi) + nt(wrl_ref[...], hi) + nt(wrh_ref[...], lo) + br_ref[...]
    expert = lax.broadcasted_iota(jnp.int32, logits.shape, 0).astype(F32)
    work = logits
    tops, picks = [], []
    for _ in range(TOP_K):
        m = jnp.max(work, axis=0, keepdims=True)
        idx = jnp.min(jnp.where(work == m, expert, float(N_EXPERTS)), axis=0, keepdims=True)
        pick = expert == idx
        tops.append(m)
        picks.append(pick)
        work = jnp.where(pick, -jnp.inf, work)
    exps = [jnp.exp(m - tops[0]) for m in tops]
    den = exps[0] + exps[1] + exps[2] + exps[3]
    gd = jnp.full(logits.shape, -1.0, F32)
    for pick, e in zip(picks, exps):
        gd = jnp.where(pick, e / den, gd)
    gd_ref[...] = gd


def _outproj(y, bv, g, ya, x2d, mods2, lw, n_seq):
    n = x2d.shape[0]
    tm = min(n, OUT_TILE)
    full = lambda shape: pl.BlockSpec(shape, lambda i: (0,) * len(shape))
    mod_spec = lambda j: pl.BlockSpec((n_seq, D_MODEL), lambda i: (0, j))
    row = lambda width: pl.BlockSpec((tm, width), lambda i: (i, 0))
    return pl.pallas_call(
        _outproj_kernel,
        grid=(n // tm,),
        in_specs=[row(D_B), row(D_B), row(D_B), row(D_A), row(D_MODEL),
                  mod_spec(2), mod_spec(3), mod_spec(4),
                  full((_N_VEC, D_B)), full((2, D_MODEL)), full((D_MODEL, D_MODEL)),
                  full((N_EXPERTS, D_MODEL)), full((N_EXPERTS, D_MODEL)), full((N_EXPERTS, 1)),
                  full((D_B, D_B))],
        out_specs=[row(D_MODEL), row(D_MODEL), pl.BlockSpec((N_EXPERTS, tm), lambda i: (0, i))],
        out_shape=[jax.ShapeDtypeStruct((n, D_MODEL), F32),
                   jax.ShapeDtypeStruct((n, D_MODEL), BF16),
                   jax.ShapeDtypeStruct((N_EXPERTS, n), F32)],
        compiler_params=_cparams("arbitrary"),
    )(y, bv, g, ya, x2d, mods2, mods2, mods2, lw["vec"], lw["ln1"], lw["w_out"],
      lw["wr_hi"], lw["wr_lo"], lw["b_router"], lw["bd"])


def _moe_kernel(h_ref, gdt_ref, u_ref, wup_ref, bup_ref, wdn_ref, bdn_ref, o_ref,
                rank_scr, p_buf, y_buf, fill_ref):
    e = pl.program_id(1)
    tm = MOE_TILE
    n_sub = h_ref.shape[0] // tm
    toks = [slice(sub * tm, (sub + 1) * tm) for sub in range(n_sub)]

    def flush(sub):
        o_ref[toks[sub], :] += lax.dot_general(p_buf[sub], y_buf[sub], (((0,), (0,)), ((), ())),
                                               preferred_element_type=F32)

    @pl.when(e == 0)
    def _():
        for sub in range(n_sub):
            o_ref[toks[sub], :] = jnp.zeros((tm, D_MODEL), F32)
            p_buf[sub] = jnp.zeros(p_buf.shape[1:], BF16)
            y_buf[sub] = jnp.zeros(y_buf.shape[1:], BF16)
            fill_ref[sub] = 0
            sel = jnp.where(gdt_ref[:, toks[sub]] >= 0.0, 1.0, 0.0).astype(BF16)
            rank_scr[sub] = _dot(sel, u_ref[...])

    def routing(sub):
        gd_row = gdt_ref[pl.ds(e, 1), toks[sub]]
        sel_row = gd_row >= 0.0
        gate_row = jnp.where(sel_row, gd_row, 0.0)
        slot_row = jnp.where(sel_row, rank_scr[sub, pl.ds(e, 1), :].astype(jnp.int32), -1)
        return sel_row, gate_row, slot_row

    def gather(sub, slot_row, first_row, n_rows):
        row_id = lax.broadcasted_iota(jnp.int32, (n_rows, tm), 0)
        onehot = jnp.where(slot_row == row_id + first_row, 1.0, 0.0)
        p = onehot.astype(BF16)
        return onehot, p, _dot(p, h_ref[toks[sub], :])

    def ffn(xg):
        hcat = _dot(xg.astype(BF16), wup_ref[0, 0]) + bup_ref[0, 0]
        glu = jnp.minimum(hcat[:, :D_FF], SWIGLU_LIMIT)
        lin = jnp.clip(hcat[:, D_FF:], -SWIGLU_LIMIT, SWIGLU_LIMIT)
        act = glu * jax.nn.sigmoid(SWIGLU_ALPHA * glu) * (lin + 1.0)
        return _dot(act.astype(BF16), wdn_ref[0, 0]) + bdn_ref[0, 0]

    def enqueue(sub, onehot, p, yb, gate_row):
        n_rows = p.shape[0]
        gate = jnp.sum(onehot * gate_row, axis=1, keepdims=True)
        fill = fill_ref[sub]
        off = pl.multiple_of(fill * MOE_ROWS, 16)
        p_buf[sub, pl.ds(off, n_rows), :] = p
        y_buf[sub, pl.ds(off, n_rows), :] = (yb * gate).astype(BF16)
        if n_rows < MOE_ROWS:
            rest = pl.ds(pl.multiple_of(off + n_rows, 16), MOE_ROWS - n_rows)
            p_buf[sub, rest, :] = jnp.zeros((MOE_ROWS - n_rows, tm), BF16)
        fill_ref[sub] = fill + 1

        @pl.when(fill + 1 == MOE_FLUSH)
        def _():
            flush(sub)
            fill_ref[sub] = 0

    routes = [routing(sub) for sub in range(n_sub)]
    gathered = [gather(sub, routes[sub][2], 0, MOE_ROWS) for sub in range(n_sub)]
    y_all = ffn(jnp.concatenate([g[2] for g in gathered], axis=0))
    for sub in range(n_sub):
        onehot, p, _ = gathered[sub]
        enqueue(sub, onehot, p, y_all[sub * MOE_ROWS:(sub + 1) * MOE_ROWS], routes[sub][1])

    for sub in range(n_sub):
        sel_row, gate_row, slot_row = routes[sub]
        count = jnp.sum(jnp.where(sel_row, 1, 0).astype(jnp.int32))
        n_extra = (jnp.maximum(count - MOE_ROWS, 0) + MOE_EXTRA - 1) // MOE_EXTRA

        def extra(b, carry, sub=sub, gate_row=gate_row, slot_row=slot_row):
            onehot, p, xg = gather(sub, slot_row, MOE_ROWS + b * MOE_EXTRA, MOE_EXTRA)
            enqueue(sub, onehot, p, ffn(xg), gate_row)
            return carry

        lax.fori_loop(0, n_extra, extra, 0)

        @pl.when(e == pl.num_programs(1) - 1)
        def _(sub=sub):
            fill = fill_ref[sub]
            for s in range(MOE_FLUSH):
                @pl.when(s >= fill)
                def _(s=s):
                    p_buf[sub, s * MOE_ROWS:(s + 1) * MOE_ROWS, :] = jnp.zeros((MOE_ROWS, tm), BF16)
            flush(sub)


def _moe(h2, gdt, lw, layer):
    n = h2.shape[0]
    rows = MOE_GROUP * MOE_TILE
    slots = MOE_FLUSH * MOE_ROWS
    return pl.pallas_call(
        _moe_kernel,
        grid=(n // rows, N_EXPERTS),
        in_specs=[pl.BlockSpec((rows, D_MODEL), lambda i, e: (i, 0)),
                  pl.BlockSpec((N_EXPERTS, rows), lambda i, e: (0, i)),
                  pl.BlockSpec((MOE_TILE, MOE_TILE), lambda i, e: (0, 0)),
                  pl.BlockSpec((1, 1, D_MODEL, 2 * D_FF), lambda i, e: (layer, e, 0, 0)),
                  pl.BlockSpec((1, 1, 1, 2 * D_FF), lambda i, e: (layer, e, 0, 0)),
                  pl.BlockSpec((1, 1, D_FF, D_MODEL), lambda i, e: (layer, e, 0, 0)),
                  pl.BlockSpec((1, 1, 1, D_MODEL), lambda i, e: (layer, e, 0, 0))],
        out_specs=pl.BlockSpec((rows, D_MODEL), lambda i, e: (i, 0)),
        out_shape=jax.ShapeDtypeStruct((n, D_MODEL), F32),
        scratch_shapes=[pltpu.VMEM((MOE_GROUP, N_EXPERTS, MOE_TILE), F32),
                        pltpu.VMEM((MOE_GROUP, slots, MOE_TILE), BF16),
                        pltpu.VMEM((MOE_GROUP, slots, D_MODEL), BF16),
                        pltpu.SMEM((MOE_GROUP,), jnp.int32)],
        compiler_params=_cparams("arbitrary", "arbitrary"),
    )(h2, gdt, lw["tri"], lw["w_up"], lw["b_up"], lw["w_down"], lw["b_down"])


def _ln2_kernel(x_ref, f_ref, g2_ref, lnd_ref, o_ref):
    lnd = lnd_ref[...]
    x = x_ref[...]
    o_ref[...] = _ln_rows(ALPHA * x + _per_seq(x, g2_ref[...]) * f_ref[...], lnd[0:1], lnd[1:2], LN_EPS)


def _ln2(x2d, ffn_all, row0, mods2, lw, n_seq):
    n = x2d.shape[0]
    tm = min(n, 512)
    blk0 = row0 // tm
    return pl.pallas_call(
        _ln2_kernel,
        grid=(n // tm,),
        in_specs=[pl.BlockSpec((tm, D_MODEL), lambda i: (i, 0)),
                  pl.BlockSpec((tm, D_MODEL), lambda i: (blk0 + i, 0)),
                  pl.BlockSpec((n_seq, D_MODEL), lambda i: (0, 5)),
                  pl.BlockSpec((2, D_MODEL), lambda i: (0, 0))],
        out_specs=pl.BlockSpec((tm, D_MODEL), lambda i: (i, 0)),
        out_shape=jax.ShapeDtypeStruct((n, D_MODEL), F32),
        compiler_params=_cparams("arbitrary"),
    )(x2d, ffn_all, mods2, lw["ln2"])


def _mixer(x2d, mods2, lw, n_seq, shift, wkv):
    if shift is None:
        ya, g, bv, r, w, k2, v, kk, akk, shift_new = _inproj(x2d, mods2, lw, n_seq)
        s0 = jnp.zeros((n_seq // SEQ_GROUP, 2, V_QUART, HEAD, LANES), F32)
        va = None
    else:
        ya, g, bv, r, w, k2, v, kk, akk, shift_new, va = _inproj(x2d, mods2, lw, n_seq, shift)
        s0 = _state_to_chain(wkv, n_seq)
    y, s_t = _recurrence(kk, w, akk, k2, r, v, s0, n_seq)
    x1, h2, gd = _outproj(y, bv, g, ya, x2d, mods2, lw, n_seq)
    return x1, h2, gd, shift_new, _state_from_chain(s_t, n_seq), va


def kernel(x_prompt, x_sample, state_wkv, state_shift, c_prompt, c_sample, ln_in_g, ln_in_b, w_ada, b_ada, w_in, ln_v_g, ln_v_b, w_s, b_s, mu_shift, w0, w_decay_up, a0, w_iclr_up, w_gate_up, k_k, k_a, r_k, ln_x_g, ln_x_b, w_out, ln1_g, ln1_b, w_router, b_router, w_up, b_up, w_down, b_down, ln2_g, ln2_b):
    bp, tp, _ = x_prompt.shape
    bs = x_sample.shape[0]
    n_p, n_s = bp * tp, bs
    moe_rows = MOE_GROUP * MOE_TILE
    n_all = -(-(n_p + n_s) // moe_rows) * moe_rows

    lane_head = jnp.arange(D_B) // HEAD
    bd = (lane_head[:, None] == lane_head[None, :]).astype(BF16)
    tri = (jnp.arange(MOE_TILE)[:, None] < jnp.arange(MOE_TILE)[None, :]).astype(BF16)
    zeros_lora = jnp.zeros((LORA_W, D_B), F32)
    w_up_bf, w_down_bf = w_up.astype(BF16), w_down.astype(BF16)
    layers = []
    for l in range(DEPTH):
        rows = [ln_v_g[l], ln_v_b[l], w0[l], a0[l], k_k[l], k_a[l], r_k[l].reshape(D_B),
                jnp.repeat(w_s[l, :, 0, 0], HEAD), jnp.repeat(b_s[l, :, 0], HEAD), ln_x_g[l], ln_x_b[l]]
        vec = jnp.stack(rows + [jnp.zeros((D_B,), F32)] * (_N_VEC - len(rows)))
        wr_t = w_router[l].T
        wr_hi = wr_t.astype(BF16)
        layers.append(dict(
            w_in=w_in[l].astype(BF16), vec=vec, mu=mu_shift[l].reshape(1, D_B_PROJ),
            wwa=jnp.concatenate([jnp.concatenate([w_decay_up[l], zeros_lora], axis=1),
                                 jnp.concatenate([zeros_lora, w_iclr_up[l]], axis=1)], axis=0).astype(BF16),
            wg=w_gate_up[l].astype(BF16), bd=bd, w_s=w_s[l],
            bsx=jnp.repeat(b_s[l].T, HEAD, axis=1),
            w_out=w_out[l].astype(BF16),
            ln1=jnp.stack([ln1_g[l], ln1_b[l]]), ln2=jnp.stack([ln2_g[l], ln2_b[l]]),
            wr_hi=wr_hi, wr_lo=(wr_t - wr_hi.astype(F32)).astype(BF16),
            b_router=b_router[l].reshape(N_EXPERTS, 1), tri=tri,
            w_up=w_up_bf, b_up=b_up.reshape(DEPTH, N_EXPERTS, 1, 2 * D_FF),
            w_down=w_down_bf, b_down=b_down.reshape(DEPTH, N_EXPERTS, 1, D_MODEL)))

    mods = _mods(jnp.concatenate([c_prompt, c_sample], axis=0), w_ada, b_ada)
    xp = _layer_norm(x_prompt.transpose(1, 0, 2).reshape(n_p, D_MODEL), ln_in_g, ln_in_b)
    xs = _layer_norm(x_sample.reshape(n_s, D_MODEL), ln_in_g, ln_in_b)

    pad = n_all - n_p - n_s
    wkv_p, shift_p, wkv_s, shift_s, v_s = [], [], [], [], []
    for l in range(DEPTH):
        lw = layers[l]
        mods_p, mods_s = mods[l, :bp], mods[l, bp:]
        x1p, h2p, gdp, sh_p, s_p, _ = _mixer(xp, mods_p, lw, bp, None, None)
        x1s, h2s, gds, sh_s, s_s, va_s = _mixer(xs, mods_s, lw, bs, state_shift[l], state_wkv[l])
        h2 = jnp.concatenate([h2p, h2s, jnp.zeros((pad, D_MODEL), BF16)], axis=0)
        gdt = jnp.concatenate([gdp, gds, jnp.full((N_EXPERTS, pad), -1.0, F32)], axis=1)
        ffn = _moe(h2, gdt, lw, l)
        xp = _ln2(x1p, ffn, 0, mods_p, lw, bp)
        xs = _ln2(x1s, ffn, n_p, mods_s, lw, bs)
        wkv_p.append(s_p)
        shift_p.append(sh_p)
        wkv_s.append(s_s)
        shift_s.append(sh_s)
        v_s.append(va_s.reshape(bs, 1, D_A))

    y_prompt = xp.reshape(tp, bp, D_MODEL).transpose(1, 0, 2)
    return (y_prompt, xs.reshape(bs, 1, D_MODEL), jnp.stack(wkv_p), jnp.stack(shift_p),
            jnp.stack(wkv_s), jnp.stack(shift_s), jnp.stack(v_s))
```

```python
import functools
import math

import jax
import jax.numpy as jnp
from jax import lax
from jax.experimental import pallas as pl
from jax.experimental.pallas import tpu as pltpu

F32 = jnp.float32
BF16 = jnp.bfloat16

D_MODEL = 1024
DEPTH = 4
D_A = 512
D_B = 512
HEAD = 64
N_HEADS = 8
CHUNK = 128
LORA_W = 64
LORA_A = 64
LORA_G = 128
D_B_PROJ = 3 * D_B + LORA_W + LORA_A + LORA_G
D_PROJ = 2 * D_A + D_B_PROJ
DECAY_SCALE = math.exp(-0.5)
GN_EPS = 64e-5
LN_EPS = 1e-5
N_EXPERTS = 32
TOP_K = 4
D_FF = D_MODEL
SWIGLU_ALPHA = 1.702
SWIGLU_LIMIT = 7.0
ALPHA = (2 * DEPTH) ** 0.25

LANES = 128
SUBLANES = 8
VMEM_LIMIT = 56 * 1024 * 1024

ROW_TILE = 256
MOE_TILE = 1024
MOE_GROUP = 3
MOE_ROWS = 144
MOE_FLUSH = 7
MOE_EXTRA = 48
OUT_TILE = 512
REC_STEPS = 128

SEQ_GROUP = SUBLANES
V_QUART = HEAD // 4
HEAD_PAIRS = N_HEADS // 2


def _cparams(*sem, flags=None):
    return pltpu.CompilerParams(dimension_semantics=sem, vmem_limit_bytes=VMEM_LIMIT, flags=flags)


def _dot(a, b):
    return jnp.dot(a, b, preferred_element_type=F32)


def _ln_rows(x, g, b, eps):
    mu = jnp.mean(x, axis=-1, keepdims=True)
    xc = x - mu
    var = jnp.mean(xc * xc, axis=-1, keepdims=True)
    return xc * lax.rsqrt(var + eps) * g + b


def _split_dot(x, w_bf16):
    hi = x.astype(BF16)
    lo = (x - hi.astype(F32)).astype(BF16)
    return _dot(hi, w_bf16) + _dot(lo, w_bf16)


def _gelu_tanh(x):
    return 0.5 * x * (1.0 + jnp.tanh(0.7978845608028654 * (x + 0.044715 * (x * x * x))))


def _per_seq(x, m):
    p = m.shape[0]
    if p == x.shape[0]:
        return m
    return jnp.broadcast_to(m[None], (x.shape[0] // p, p, m.shape[1])).reshape(x.shape[0], m.shape[1])


def _ln_kernel(x_ref, g_ref, b_ref, o_ref):
    o_ref[...] = _ln_rows(x_ref[...], g_ref[...], b_ref[...], LN_EPS)


def _layer_norm(x2d, g, b):
    n = x2d.shape[0]
    tm = min(n, 512)
    return pl.pallas_call(
        _ln_kernel,
        grid=(n // tm,),
        in_specs=[pl.BlockSpec((tm, D_MODEL), lambda i: (i, 0)),
                  pl.BlockSpec((1, D_MODEL), lambda i: (0, 0)),
                  pl.BlockSpec((1, D_MODEL), lambda i: (0, 0))],
        out_specs=pl.BlockSpec((tm, D_MODEL), lambda i: (i, 0)),
        out_shape=jax.ShapeDtypeStruct((n, D_MODEL), F32),
        compiler_params=_cparams("arbitrary"),
    )(x2d, g.reshape(1, D_MODEL), b.reshape(1, D_MODEL))


def _mods_kernel(c_ref, w_ref, b_ref, o_ref):
    c = c_ref[...]
    ca = c * jax.nn.sigmoid(c)
    o_ref[0] = _dot(ca.astype(BF16), w_ref[0].astype(BF16)) + b_ref[0]


def _mods(c_all, w_ada, b_ada):
    n = c_all.shape[0]
    tn = 1536
    return pl.pallas_call(
        _mods_kernel,
        grid=(DEPTH, 6 * D_MODEL // tn),
        in_specs=[pl.BlockSpec((n, D_MODEL), lambda l, j: (0, 0)),
                  pl.BlockSpec((1, D_MODEL, tn), lambda l, j: (l, 0, j)),
                  pl.BlockSpec((1, 1, tn), lambda l, j: (l, 0, j))],
        out_specs=pl.BlockSpec((1, n, tn), lambda l, j: (l, 0, j)),
        out_shape=jax.ShapeDtypeStruct((DEPTH, n, 6 * D_MODEL), F32),
        compiler_params=_cparams("arbitrary", "arbitrary"),
    )(c_all, w_ada, b_ada.reshape(DEPTH, 1, 6 * D_MODEL))


_V_LNV_G, _V_LNV_B, _V_W0, _V_A0, _V_KK, _V_KA, _V_RK, _V_W00, _V_B0, _V_LNX_G, _V_LNX_B = range(11)
_N_VEC = 16


def _project(x, sh, sc, win_ref):
    h = x * (1.0 + _per_seq(x, sc)) + _per_seq(x, sh)
    return _dot(h.astype(BF16), win_ref[...])


def _gelu_split(proj, vec):
    ga = _gelu_tanh(proj[:, :2 * D_A])
    va = _ln_rows(ga[:, D_A:], vec[_V_LNV_G:_V_LNV_G + 1], vec[_V_LNV_B:_V_LNV_B + 1], LN_EPS)
    return ga[:, :D_A], va


def _rwkv_prep(pb, prev, vec, mu_ref, wwa_ref, wg_ref, bd_ref, outs, rows):
    g_ref, bv_ref, r_ref, w_ref, k_ref, v_ref, kk_ref, akk_ref = outs
    row = lambda i: vec[i:i + 1, :]
    xs = pb + mu_ref[...] * (prev - pb)
    r = xs[:, :D_B]
    k = xs[:, D_B:2 * D_B]
    v = xs[:, 2 * D_B:3 * D_B]
    xwa = xs[:, 3 * D_B:3 * D_B + LORA_W + LORA_A]
    xg = xs[:, 3 * D_B + LORA_W + LORA_A:]
    lane = lax.broadcasted_iota(jnp.int32, (1, LORA_W + LORA_A), 1)
    lora_in = jnp.where(lane < LORA_W, jnp.tanh(xwa), xwa)
    wa = _dot(lora_in.astype(BF16), wwa_ref[...])
    w = jnp.exp(-DECAY_SCALE * jax.nn.sigmoid(row(_V_W0) + wa[:, :D_B]))
    a = jax.nn.sigmoid(row(_V_A0) + wa[:, D_B:])
    g = _dot(jax.nn.sigmoid(xg).astype(BF16), wg_ref[...])
    bd = bd_ref[...]
    kk = k * row(_V_KK)
    kk = kk * lax.rsqrt(jnp.maximum(_split_dot(kk * kk, bd), 1e-24))
    k2 = k * (1.0 + (a - 1.0) * row(_V_KA))
    bonus = _split_dot(r * k2 * row(_V_RK), bd)
    g_ref[rows, :] = g
    bv_ref[rows, :] = bonus * v
    r_ref[rows, :] = r
    w_ref[rows, :] = w
    k_ref[rows, :] = k2
    v_ref[rows, :] = v
    kk_ref[rows, :] = kk
    akk_ref[rows, :] = kk * a


def _inproj_seq_kernel(x_ref, sh_ref, sc_ref, win_ref, vec_ref, mu_ref, wwa_ref, wg_ref, bd_ref,
                       ws_ref, bsx_ref,
                       ya_ref, g_ref, bv_ref, r_ref, w_ref, k_ref, v_ref, kk_ref, akk_ref, shift_ref,
                       carry_ref, ua_scr, va_scr, ya_scr):
    n_seq = sh_ref.shape[0]
    sub_rows = x_ref.shape[0]
    sb = pl.program_id(1)
    vec = vec_ref[...]
    outs = (g_ref, bv_ref, r_ref, w_ref, k_ref, v_ref, kk_ref, akk_ref)

    @pl.when((pl.program_id(0) == 0) & (sb == 0))
    def _():
        carry_ref[...] = jnp.zeros_like(carry_ref)

    rows = pl.ds(pl.multiple_of(sb * sub_rows, sub_rows), sub_rows)
    proj = _project(x_ref[...], sh_ref[...], sc_ref[...], win_ref)
    ua, va = _gelu_split(proj, vec)
    for j in range(D_A // LANES):
        ua_scr[j, rows, :] = ua[:, LANES * j:LANES * (j + 1)]
        va_scr[j, rows, :] = va[:, LANES * j:LANES * (j + 1)]
    pb = proj[:, 2 * D_A:]
    prev = jnp.concatenate([carry_ref[...], pb[:sub_rows - n_seq, :]], axis=0)
    carry_ref[...] = pb[sub_rows - n_seq:, :]
    shift_ref[...] = pb[sub_rows - n_seq:, :]
    _rwkv_prep(pb, prev, vec, mu_ref, wwa_ref, wg_ref, bd_ref, outs, slice(None))

    @pl.when(sb == pl.num_programs(1) - 1)
    def _():
        head_of_lane = lax.broadcasted_iota(jnp.int32, (1, D_A), 1) // HEAD
        ri = lax.broadcasted_iota(jnp.int32, (CHUNK, CHUNK), 0)
        ci = lax.broadcasted_iota(jnp.int32, (CHUNK, CHUNK), 1)
        w_causal = [jnp.where(ri >= ci, ws_ref[hd], 0.0).astype(BF16) for hd in range(N_HEADS)]
        lane_blocks = range(D_A // LANES)
        for b in range(n_seq):
            seq_rows = pl.ds(b, CHUNK, stride=n_seq)
            vb = jnp.concatenate([va_scr[j, seq_rows, :] for j in lane_blocks], axis=1)
            ub = jnp.concatenate([ua_scr[j, seq_rows, :] for j in lane_blocks], axis=1)
            s = bsx_ref[...]
            for hd in range(N_HEADS):
                v_h = jnp.where(head_of_lane == hd, vb, 0.0).astype(BF16)
                s = s + _dot(w_causal[hd], v_h)
            ya = ub * s
            for j in lane_blocks:
                ya_scr[j, seq_rows, :] = ya[:, LANES * j:LANES * (j + 1)]
        ya_ref[...] = jnp.concatenate([ya_scr[j] for j in lane_blocks], axis=1).astype(ya_ref.dtype)


def _inproj_step_kernel(x_ref, sh_ref, sc_ref, win_ref, vec_ref, mu_ref, wwa_ref, wg_ref, bd_ref,
                        prev_ref,
                        ya_ref, g_ref, bv_ref, r_ref, w_ref, k_ref, v_ref, kk_ref, akk_ref, shift_ref,
                        va_ref):
    vec = vec_ref[...]
    outs = (g_ref, bv_ref, r_ref, w_ref, k_ref, v_ref, kk_ref, akk_ref)
    proj = _project(x_ref[...], sh_ref[...], sc_ref[...], win_ref)
    ua, va = _gelu_split(proj, vec)
    va_ref[...] = va
    ya_ref[...] = (ua * (va * vec[_V_W00:_V_W00 + 1] + vec[_V_B0:_V_B0 + 1])).astype(ya_ref.dtype)
    pb = proj[:, 2 * D_A:]
    shift_ref[...] = pb
    _rwkv_prep(pb, prev_ref[...], vec, mu_ref, wwa_ref, wg_ref, bd_ref, outs, slice(None))


def _inproj(x2d, mods2, lw, n_seq, prev=None):
    n = x2d.shape[0]
    seq_mode = prev is None
    chunk_rows = CHUNK * n_seq
    tile = ROW_TILE if seq_mode else n
    n_sub = chunk_rows // tile if seq_mode else 1
    full = lambda shape: pl.BlockSpec(shape, lambda i, s: (0,) * len(shape))
    mod_spec = lambda j: pl.BlockSpec((n_seq, D_MODEL), lambda i, s: (0, j))
    row_spec = lambda width: pl.BlockSpec((tile, width), lambda i, s: (i * n_sub + s, 0))
    in_specs = [row_spec(D_MODEL), mod_spec(0), mod_spec(1),
                full((D_MODEL, D_PROJ)), full((_N_VEC, D_B)), full((1, D_B_PROJ)),
                full((LORA_W + LORA_A, 2 * D_B)), full((LORA_G, D_B)), full((D_B, D_B))]
    args = [x2d, mods2, mods2, lw["w_in"], lw["vec"], lw["mu"], lw["wwa"], lw["wg"], lw["bd"]]
    row_shape = jax.ShapeDtypeStruct((n, D_B), F32)
    out_shape = [jax.ShapeDtypeStruct((n, D_A), BF16)] + [row_shape] * 8
    if seq_mode:
        in_specs += [full((N_HEADS, CHUNK, CHUNK)), full((CHUNK, D_A))]
        args += [lw["w_s"], lw["bsx"]]
        out_specs = [pl.BlockSpec((chunk_rows, D_A), lambda i, s: (i, 0))] + [row_spec(D_B)] * 8
        out_specs += [full((n_seq, D_B_PROJ))]
        out_shape += [jax.ShapeDtypeStruct((n_seq, D_B_PROJ), F32)]
        scratch = ([pltpu.VMEM((n_seq, D_B_PROJ), F32)]
                   + [pltpu.VMEM((D_A // LANES, chunk_rows, LANES), F32)] * 3)
        body = _inproj_seq_kernel
    else:
        out_specs = [row_spec(D_A)] + [row_spec(D_B)] * 8
        in_specs += [row_spec(D_B_PROJ)]
        args += [prev]
        out_specs += [row_spec(D_B_PROJ), row_spec(D_A)]
        out_shape += [jax.ShapeDtypeStruct((n, D_B_PROJ), F32), row_shape]
        scratch = []
        body = _inproj_step_kernel
    return pl.pallas_call(
        body, grid=(n // (tile * n_sub), n_sub),
        in_specs=in_specs, out_specs=out_specs, out_shape=out_shape, scratch_shapes=scratch,
        compiler_params=_cparams("arbitrary", "arbitrary"),
    )(*args)


def _rec_kernel(kk_ref, w_ref, akk_ref, k_ref, r_ref, v_ref, s0_ref, y_ref, st_ref,
                s_scr, op_scr0, op_scr1, v_scr0, v_scr1, y_scr0, y_scr1):
    tc = pl.program_id(1)
    steps = kk_ref.shape[0]

    @pl.when(tc == 0)
    def _():
        s_scr[...] = s0_ref[0]
        y_scr0[...] = jnp.zeros_like(y_scr0)
        y_scr1[...] = jnp.zeros_like(y_scr1)

    op_scr, v_scr, y_scr = (op_scr0, op_scr1), (v_scr0, v_scr1), (y_scr0, y_scr1)

    def stacked(x):
        return jnp.concatenate([x[:, LANES * j:LANES * (j + 1)] for j in range(HEAD_PAIRS)], axis=0)

    op_refs = (kk_ref, w_ref, akk_ref, k_ref, r_ref)
    value_quarter = (lax.broadcasted_iota(jnp.int32, (1, LANES), 1) % HEAD) // V_QUART

    def prep_op(t, buf, i):
        m = stacked(op_refs[i][t])
        op_scr[buf][i] = jnp.concatenate([m] * 4, axis=0).T

    def prep_v(t, buf):
        mv = stacked(v_ref[t])
        v_scr[buf][...] = jnp.concatenate(
            [mv if q == 0 else pltpu.roll(mv, LANES - V_QUART * q, 1) for q in range(4)], axis=0).T

    def update(buf, part, n_parts):
        items = [(par, vi) for par in range(2) for vi in range(V_QUART)]
        per = len(items) // n_parts
        for par, vi in items[part * per:(part + 1) * per]:
            keys = slice(par * HEAD, (par + 1) * HEAD)
            row = par * HEAD + vi
            s_old = s_scr[par, vi]
            ops = op_scr[buf]
            skk = jnp.sum(s_old * ops[0, keys, :], axis=0, keepdims=True)
            s_new = s_old * ops[1, keys, :] + (
                v_scr[buf][row:row + 1, :] * ops[3, keys, :] - skk * ops[2, keys, :])
            s_scr[par, vi] = s_new
            y_row = jnp.sum(s_new * ops[4, keys, :], axis=0, keepdims=True)
            for q in range(4):
                y_scr[buf][row + V_QUART * q:row + V_QUART * q + 1, :] = y_row

    def out_rows(buf):
        yt = y_scr[buf][...].T
        acc = jnp.where(value_quarter == 0, yt[0:32], 0.0)
        for q in range(1, 4):
            acc = acc + jnp.where(value_quarter == q, yt[32 * q:32 * (q + 1)], 0.0)
        return jnp.concatenate([acc[SEQ_GROUP * j:SEQ_GROUP * (j + 1)] for j in range(HEAD_PAIRS)], axis=1)

    def emit(t, buf):
        y_ref[t] = out_rows(buf)

    if steps % 2:
        def step(t, carry):
            for i in range(len(op_refs)):
                prep_op(t, 0, i)
            prep_v(t, 0)
            update(0, 0, 1)
            emit(t, 0)
            return carry

        lax.fori_loop(0, steps, step, 0)
    else:
        for i in range(len(op_refs)):
            prep_op(0, 0, i)
        prep_v(0, 0)

        def half(t_cur, t_next, cur, nxt):
            rows_prev = out_rows(nxt)
            prep_op(t_next, nxt, 0)
            update(cur, 0, 4)
            prep_op(t_next, nxt, 1)
            prep_op(t_next, nxt, 2)
            update(cur, 1, 4)
            prep_op(t_next, nxt, 3)
            prep_op(t_next, nxt, 4)
            update(cur, 2, 4)
            prep_v(t_next, nxt)
            update(cur, 3, 4)
            y_ref[jnp.maximum(t_cur - 1, 0)] = rows_prev

        def pair(i, carry):
            t0 = 2 * i
            half(t0, t0 + 1, 0, 1)
            half(t0 + 1, jnp.minimum(t0 + 2, steps - 1), 1, 0)
            return carry

        lax.fori_loop(0, steps // 2, pair, 0)
        emit(steps - 1, 1)

    @pl.when(tc == pl.num_programs(1) - 1)
    def _():
        st_ref[0] = s_scr[...]


def _recurrence(kk, w, akk, k, r, v, s0, n_seq):
    t = kk.shape[0] // n_seq
    groups = n_seq // SEQ_GROUP
    tc = min(t, REC_STEPS)
    as3 = lambda z: z.reshape(t, n_seq, D_B)
    spec = pl.BlockSpec((tc, SEQ_GROUP, D_B), lambda g, i: (i, g, 0))
    sspec = pl.BlockSpec((1, 2, V_QUART, HEAD, LANES), lambda g, i: (g, 0, 0, 0, 0))
    y, s_t = pl.pallas_call(
        _rec_kernel,
        grid=(groups, t // tc),
        in_specs=[spec] * 6 + [sspec],
        out_specs=[spec, sspec],
        out_shape=[jax.ShapeDtypeStruct((t, n_seq, D_B), F32),
                   jax.ShapeDtypeStruct((groups, 2, V_QUART, HEAD, LANES), F32)],
        scratch_shapes=[pltpu.VMEM((2, V_QUART, HEAD, LANES), F32),
                        pltpu.VMEM((5, LANES, LANES), F32), pltpu.VMEM((5, LANES, LANES), F32),
                        pltpu.VMEM((LANES, LANES), F32), pltpu.VMEM((LANES, LANES), F32),
                        pltpu.VMEM((LANES, LANES), F32), pltpu.VMEM((LANES, LANES), F32)],
        compiler_params=_cparams("arbitrary", "arbitrary"),
    )(as3(kk), as3(w), as3(akk), as3(k), as3(r), as3(v), s0)
    return y.reshape(t * n_seq, D_B), s_t


def _state_to_chain(s, n_seq):
    g = n_seq // SEQ_GROUP
    s = s.reshape(g, SEQ_GROUP, HEAD_PAIRS, 2, 4, V_QUART, HEAD).transpose(0, 3, 5, 6, 4, 2, 1)
    return s.reshape(g, 2, V_QUART, HEAD, LANES)


def _state_from_chain(sc, n_seq):
    g = n_seq // SEQ_GROUP
    s = sc.reshape(g, 2, V_QUART, HEAD, 4, HEAD_PAIRS, SEQ_GROUP).transpose(0, 6, 5, 1, 4, 2, 3)
    return s.reshape(n_seq, N_HEADS, HEAD, HEAD)


def _outproj_kernel(y_ref, bv_ref, g_ref, ya_ref, x_ref, g1_ref, sh2_ref, sc2_ref, vec_ref, lnd_ref,
                    wout_ref, wrh_ref, wrl_ref, br_ref, bd_ref, x1_ref, h2_ref, gd_ref):
    vec = vec_ref[...]
    bd = bd_ref[...]
    y = y_ref[...]
    mean = _split_dot(y, bd) * (1.0 / HEAD)
    yc = y - mean
    var = _split_dot(yc * yc, bd) * (1.0 / HEAD)
    yn = yc * lax.rsqrt(var + GN_EPS) * vec[_V_LNX_G:_V_LNX_G + 1] + vec[_V_LNX_B:_V_LNX_B + 1]
    yb = (yn + bv_ref[...]) * g_ref[...]
    mix = _dot(ya_ref[...], wout_ref[:D_A, :]) + _dot(yb.astype(BF16), wout_ref[D_A:, :])
    lnd = lnd_ref[...]
    x = x_ref[...]
    x1 = _ln_rows(ALPHA * x + _per_seq(x, g1_ref[...]) * mix, lnd[0:1], lnd[1:2], LN_EPS)
    x1_ref[...] = x1
    h2 = x1 * (1.0 + _per_seq(x, sc2_ref[...])) + _per_seq(x, sh2_ref[...])
    h2_ref[...] = h2.astype(h2_ref.dtype)

    hi = h2.astype(BF16)
    lo = (h2 - hi.astype(F32)).astype(BF16)
    nt = lambda w, x: lax.dot_general(w, x, (((1,), (1,)), ((), ())), preferred_element_type=F32)
    logits = nt(wrh_ref[...], hi) + nt(wrl_ref[...], hi) + nt(wrh_ref[...], lo) + br_ref[...]
    expert = lax.broadcasted_iota(jnp.int32, logits.shape, 0).astype(F32)
    work = logits
    tops, picks = [], []
    for _ in range(TOP_K):
        m = jnp.max(work, axis=0, keepdims=True)
        idx = jnp.min(jnp.where(work == m, expert, float(N_EXPERTS)), axis=0, keepdims=True)
        pick = expert == idx
        tops.append(m)
        picks.append(pick)
        work = jnp.where(pick, -jnp.inf, work)
    exps = [jnp.exp(m - tops[0]) for m in tops]
    den = exps[0] + exps[1] + exps[2] + exps[3]
    gd = jnp.full(logits.shape, -1.0, F32)
    for pick, e in zip(picks, exps):
        gd = jnp.where(pick, e / den, gd)
    gd_ref[...] = gd


def _outproj(y, bv, g, ya, x2d, mods2, lw, n_seq):
    n = x2d.shape[0]
    tm = min(n, OUT_TILE)
    full = lambda shape: pl.BlockSpec(shape, lambda i: (0,) * len(shape))
    mod_spec = lambda j: pl.BlockSpec((n_seq, D_MODEL), lambda i: (0, j))
    row = lambda width: pl.BlockSpec((tm, width), lambda i: (i, 0))
    return pl.pallas_call(
        _outproj_kernel,
        grid=(n // tm,),
        in_specs=[row(D_B), row(D_B), row(D_B), row(D_A), row(D_MODEL),
                  mod_spec(2), mod_spec(3), mod_spec(4),
                  full((_N_VEC, D_B)), full((2, D_MODEL)), full((D_MODEL, D_MODEL)),
                  full((N_EXPERTS, D_MODEL)), full((N_EXPERTS, D_MODEL)), full((N_EXPERTS, 1)),
                  full((D_B, D_B))],
        out_specs=[row(D_MODEL), row(D_MODEL), pl.BlockSpec((N_EXPERTS, tm), lambda i: (0, i))],
        out_shape=[jax.ShapeDtypeStruct((n, D_MODEL), F32),
                   jax.ShapeDtypeStruct((n, D_MODEL), BF16),
                   jax.ShapeDtypeStruct((N_EXPERTS, n), F32)],
        compiler_params=_cparams("arbitrary"),
    )(y, bv, g, ya, x2d, mods2, mods2, mods2, lw["vec"], lw["ln1"], lw["w_out"],
      lw["wr_hi"], lw["wr_lo"], lw["b_router"], lw["bd"])


def _moe_kernel(h_ref, gdt_ref, u_ref, wup_ref, bup_ref, wdn_ref, bdn_ref, o_ref,
                rank_scr, p_buf, y_buf, fill_ref):
    e = pl.program_id(1)
    tm = MOE_TILE
    n_sub = h_ref.shape[0] // tm
    toks = [slice(sub * tm, (sub + 1) * tm) for sub in range(n_sub)]

    def flush(sub):
        o_ref[toks[sub], :] += lax.dot_general(p_buf[sub], y_buf[sub], (((0,), (0,)), ((), ())),
                                               preferred_element_type=F32)

    @pl.when(e == 0)
    def _():
        for sub in range(n_sub):
            o_ref[toks[sub], :] = jnp.zeros((tm, D_MODEL), F32)
            p_buf[sub] = jnp.zeros(p_buf.shape[1:], BF16)
            y_buf[sub] = jnp.zeros(y_buf.shape[1:], BF16)
            fill_ref[sub] = 0
            sel = jnp.where(gdt_ref[:, toks[sub]] >= 0.0, 1.0, 0.0).astype(BF16)
            rank_scr[sub] = _dot(sel, u_ref[...])

    def routing(sub):
        gd_row = gdt_ref[pl.ds(e, 1), toks[sub]]
        sel_row = gd_row >= 0.0
        gate_row = jnp.where(sel_row, gd_row, 0.0)
        slot_row = jnp.where(sel_row, rank_scr[sub, pl.ds(e, 1), :].astype(jnp.int32), -1)
        return sel_row, gate_row, slot_row

    def gather(sub, slot_row, first_row, n_rows):
        row_id = lax.broadcasted_iota(jnp.int32, (n_rows, tm), 0)
        onehot = jnp.where(slot_row == row_id + first_row, 1.0, 0.0)
        p = onehot.astype(BF16)
        return onehot, p, _dot(p, h_ref[toks[sub], :])

    def ffn(xg):
        hcat = _dot(xg.astype(BF16), wup_ref[0, 0]) + bup_ref[0, 0]
        glu = jnp.minimum(hcat[:, :D_FF], SWIGLU_LIMIT)
        lin = jnp.clip(hcat[:, D_FF:], -SWIGLU_LIMIT, SWIGLU_LIMIT)
        act = glu * jax.nn.sigmoid(SWIGLU_ALPHA * glu) * (lin + 1.0)
        return _dot(act.astype(BF16), wdn_ref[0, 0]) + bdn_ref[0, 0]

    def enqueue(sub, onehot, p, yb, gate_row):
        n_rows = p.shape[0]
        gate = jnp.sum(onehot * gate_row, axis=1, keepdims=True)
        fill = fill_ref[sub]
        off = pl.multiple_of(fill * MOE_ROWS, 16)
        p_buf[sub, pl.ds(off, n_rows), :] = p
        y_buf[sub, pl.ds(off, n_rows), :] = (yb * gate).astype(BF16)
        if n_rows < MOE_ROWS:
            rest = pl.ds(pl.multiple_of(off + n_rows, 16), MOE_ROWS - n_rows)
            p_buf[sub, rest, :] = jnp.zeros((MOE_ROWS - n_rows, tm), BF16)
        fill_ref[sub] = fill + 1

        @pl.when(fill + 1 == MOE_FLUSH)
        def _():
            flush(sub)
            fill_ref[sub] = 0

    routes = [routing(sub) for sub in range(n_sub)]
    gathered = [gather(sub, routes[sub][2], 0, MOE_ROWS) for sub in range(n_sub)]
    y_all = ffn(jnp.concatenate([g[2] for g in gathered], axis=0))
    for sub in range(n_sub):
        onehot, p, _ = gathered[sub]
        enqueue(sub, onehot, p, y_all[sub * MOE_ROWS:(sub + 1) * MOE_ROWS], routes[sub][1])

    for sub in range(n_sub):
        sel_row, gate_row, slot_row = routes[sub]
        count = jnp.sum(jnp.where(sel_row, 1, 0).astype(jnp.int32))
        n_extra = (jnp.maximum(count - MOE_ROWS, 0) + MOE_EXTRA - 1) // MOE_EXTRA

        def extra(b, carry, sub=sub, gate_row=gate_row, slot_row=slot_row):
            onehot, p, xg = gather(sub, slot_row, MOE_ROWS + b * MOE_EXTRA, MOE_EXTRA)
            enqueue(sub, onehot, p, ffn(xg), gate_row)
            return carry

        lax.fori_loop(0, n_extra, extra, 0)

        @pl.when(e == pl.num_programs(1) - 1)
        def _(sub=sub):
            fill = fill_ref[sub]
            for s in range(MOE_FLUSH):
                @pl.when(s >= fill)
                def _(s=s):
                    p_buf[sub, s * MOE_ROWS:(s + 1) * MOE_ROWS, :] = jnp.zeros((MOE_ROWS, tm), BF16)
            flush(sub)


def _moe(h2, gdt, lw, layer):
    n = h2.shape[0]
    rows = MOE_GROUP * MOE_TILE
    slots = MOE_FLUSH * MOE_ROWS
    return pl.pallas_call(
        _moe_kernel,
        grid=(n // rows, N_EXPERTS),
        in_specs=[pl.BlockSpec((rows, D_MODEL), lambda i, e: (i, 0), pipeline_mode=pl.Buffered(1)),
                  pl.BlockSpec((N_EXPERTS, rows), lambda i, e: (0, i)),
                  pl.BlockSpec((MOE_TILE, MOE_TILE), lambda i, e: (0, 0)),
                  pl.BlockSpec((1, 1, D_MODEL, 2 * D_FF), lambda i, e: (layer, e, 0, 0)),
                  pl.BlockSpec((1, 1, 1, 2 * D_FF), lambda i, e: (layer, e, 0, 0)),
                  pl.BlockSpec((1, 1, D_FF, D_MODEL), lambda i, e: (layer, e, 0, 0)),
                  pl.BlockSpec((1, 1, 1, D_MODEL), lambda i, e: (layer, e, 0, 0))],
        out_specs=pl.BlockSpec((rows, D_MODEL), lambda i, e: (i, 0), pipeline_mode=pl.Buffered(1)),
        out_shape=jax.ShapeDtypeStruct((n, D_MODEL), F32),
        scratch_shapes=[pltpu.VMEM((MOE_GROUP, N_EXPERTS, MOE_TILE), F32),
                        pltpu.VMEM((MOE_GROUP, slots, MOE_TILE), BF16),
                        pltpu.VMEM((MOE_GROUP, slots, D_MODEL), BF16),
                        pltpu.SMEM((MOE_GROUP,), jnp.int32)],
        compiler_params=_cparams("arbitrary", "arbitrary"),
    )(h2, gdt, lw["tri"], lw["w_up"], lw["b_up"], lw["w_down"], lw["b_down"])


def _ln2_kernel(x_ref, f_ref, g2_ref, lnd_ref, o_ref):
    lnd = lnd_ref[...]
    x = x_ref[...]
    o_ref[...] = _ln_rows(ALPHA * x + _per_seq(x, g2_ref[...]) * f_ref[...], lnd[0:1], lnd[1:2], LN_EPS)


def _ln2(x2d, ffn_all, row0, mods2, lw, n_seq):
    n = x2d.shape[0]
    tm = min(n, 512)
    blk0 = row0 // tm
    return pl.pallas_call(
        _ln2_kernel,
        grid=(n // tm,),
        in_specs=[pl.BlockSpec((tm, D_MODEL), lambda i: (i, 0)),
                  pl.BlockSpec((tm, D_MODEL), lambda i: (blk0 + i, 0)),
                  pl.BlockSpec((n_seq, D_MODEL), lambda i: (0, 5)),
                  pl.BlockSpec((2, D_MODEL), lambda i: (0, 0))],
        out_specs=pl.BlockSpec((tm, D_MODEL), lambda i: (i, 0)),
        out_shape=jax.ShapeDtypeStruct((n, D_MODEL), F32),
        compiler_params=_cparams("arbitrary"),
    )(x2d, ffn_all, mods2, lw["ln2"])


def _mixer(x2d, mods2, lw, n_seq, shift, wkv):
    if shift is None:
        ya, g, bv, r, w, k2, v, kk, akk, shift_new = _inproj(x2d, mods2, lw, n_seq)
        s0 = jnp.zeros((n_seq // SEQ_GROUP, 2, V_QUART, HEAD, LANES), F32)
        va = None
    else:
        ya, g, bv, r, w, k2, v, kk, akk, shift_new, va = _inproj(x2d, mods2, lw, n_seq, shift)
        s0 = _state_to_chain(wkv, n_seq)
    y, s_t = _recurrence(kk, w, akk, k2, r, v, s0, n_seq)
    x1, h2, gd = _outproj(y, bv, g, ya, x2d, mods2, lw, n_seq)
    return x1, h2, gd, shift_new, _state_from_chain(s_t, n_seq), va


def kernel(x_prompt, x_sample, state_wkv, state_shift, c_prompt, c_sample, ln_in_g, ln_in_b, w_ada, b_ada, w_in, ln_v_g, ln_v_b, w_s, b_s, mu_shift, w0, w_decay_up, a0, w_iclr_up, w_gate_up, k_k, k_a, r_k, ln_x_g, ln_x_b, w_out, ln1_g, ln1_b, w_router, b_router, w_up, b_up, w_down, b_down, ln2_g, ln2_b):
    bp, tp, _ = x_prompt.shape
    bs = x_sample.shape[0]
    n_p, n_s = bp * tp, bs
    moe_rows = MOE_GROUP * MOE_TILE
    n_all = -(-(n_p + n_s) // moe_rows) * moe_rows

    lane_head = jnp.arange(D_B) // HEAD
    bd = (lane_head[:, None] == lane_head[None, :]).astype(BF16)
    tri = (jnp.arange(MOE_TILE)[:, None] < jnp.arange(MOE_TILE)[None, :]).astype(BF16)
    zeros_lora = jnp.zeros((LORA_W, D_B), F32)
    w_up_bf, w_down_bf = w_up.astype(BF16), w_down.astype(BF16)
    layers = []
    for l in range(DEPTH):
        rows = [ln_v_g[l], ln_v_b[l], w0[l], a0[l], k_k[l], k_a[l], r_k[l].reshape(D_B),
                jnp.repeat(w_s[l, :, 0, 0], HEAD), jnp.repeat(b_s[l, :, 0], HEAD), ln_x_g[l], ln_x_b[l]]
        vec = jnp.stack(rows + [jnp.zeros((D_B,), F32)] * (_N_VEC - len(rows)))
        wr_t = w_router[l].T
        wr_hi = wr_t.astype(BF16)
        layers.append(dict(
            w_in=w_in[l].astype(BF16), vec=vec, mu=mu_shift[l].reshape(1, D_B_PROJ),
            wwa=jnp.concatenate([jnp.concatenate([w_decay_up[l], zeros_lora], axis=1),
                                 jnp.concatenate([zeros_lora, w_iclr_up[l]], axis=1)], axis=0).astype(BF16),
            wg=w_gate_up[l].astype(BF16), bd=bd, w_s=w_s[l],
            bsx=jnp.repeat(b_s[l].T, HEAD, axis=1),
            w_out=w_out[l].astype(BF16),
            ln1=jnp.stack([ln1_g[l], ln1_b[l]]), ln2=jnp.stack([ln2_g[l], ln2_b[l]]),
            wr_hi=wr_hi, wr_lo=(wr_t - wr_hi.astype(F32)).astype(BF16),
            b_router=b_router[l].reshape(N_EXPERTS, 1), tri=tri,
            w_up=w_up_bf, b_up=b_up.reshape(DEPTH, N_EXPERTS, 1, 2 * D_FF),
            w_down=w_down_bf, b_down=b_down.reshape(DEPTH, N_EXPERTS, 1, D_MODEL)))

    mods = _mods(jnp.concatenate([c_prompt, c_sample], axis=0), w_ada, b_ada)
    xp = _layer_norm(x_prompt.transpose(1, 0, 2).reshape(n_p, D_MODEL), ln_in_g, ln_in_b)
    xs = _layer_norm(x_sample.reshape(n_s, D_MODEL), ln_in_g, ln_in_b)

    pad = n_all - n_p - n_s
    wkv_p, shift_p, wkv_s, shift_s, v_s = [], [], [], [], []
    for l in range(DEPTH):
        lw = layers[l]
        mods_p, mods_s = mods[l, :bp], mods[l, bp:]
        x1p, h2p, gdp, sh_p, s_p, _ = _mixer(xp, mods_p, lw, bp, None, None)
        x1s, h2s, gds, sh_s, s_s, va_s = _mixer(xs, mods_s, lw, bs, state_shift[l], state_wkv[l])
        h2 = jnp.concatenate([h2p, h2s, jnp.zeros((pad, D_MODEL), BF16)], axis=0)
        gdt = jnp.concatenate([gdp, gds, jnp.full((N_EXPERTS, pad), -1.0, F32)], axis=1)
        ffn = _moe(h2, gdt, lw, l)
        xp = _ln2(x1p, ffn, 0, mods_p, lw, bp)
        xs = _ln2(x1s, ffn, n_p, mods_s, lw, bs)
        wkv_p.append(s_p)
        shift_p.append(sh_p)
        wkv_s.append(s_s)
        shift_s.append(sh_s)
        v_s.append(va_s.reshape(bs, 1, D_A))

    y_prompt = xp.reshape(tp, bp, D_MODEL).transpose(1, 0, 2)
    return (y_prompt, xs.reshape(bs, 1, D_MODEL), jnp.stack(wkv_p), jnp.stack(shift_p),
            jnp.stack(wkv_s), jnp.stack(shift_s), jnp.stack(v_s))
```
